```python
import math
import jax, jax.numpy as jnp
from jax import lax
import numpy as np

D_MODEL = 2048
BATCH = 4
SEQ = 2048
DEPTH = 2

GRID_W = 64
CTX_LEN = 256
MLA_HEADS = 8
MLA_Q_RANK = 768
MLA_KV_RANK = 512
MLA_NOPE = 128
MLA_ROPE = 64
MLA_V = 128
DIFF_HEADS = 4
DIFF_HD = 64
DIFF_V = 2 * DIFF_HD
POOL_WINDOWS = (2, 4, 8, 16)
POOL_GROUP = 128
POOL_W = len(POOL_WINDOWS) * POOL_GROUP
N_BRANCH = 3
SPLIT_WIDTHS = (MLA_KV_RANK, MLA_ROPE, DIFF_HEADS * 2 * DIFF_HD, DIFF_HEADS * DIFF_V,
                MLA_Q_RANK, DIFF_HEADS * 2 * DIFF_HD, POOL_W, N_BRANCH * D_MODEL)
KV_COLS = sum(SPLIT_WIDTHS[:4])
IN_COLS = sum(SPLIT_WIDTHS)
N_GROUPS = 8
EXPERTS_PER_GROUP = 8
N_EXPERTS = N_GROUPS * EXPERTS_PER_GROUP
TOP_K = 2
D_EXPERT = 512
EXPERT_BLOCK = 128
Q_BLOCK = 128
ROPE_BASE = 10000.0
EPS = 1e-6
DN_ALPHA = (2 * DEPTH) ** 0.25
DN_BETA = (8 * DEPTH) ** -0.25

kernel_name = "hybrid_mla_diff_pool_hmoe_dit"


def split_cols(p, widths):
    offs = np.cumsum(widths)[:-1].tolist()
    return jnp.split(p, offs, axis=-1)


def layer_norm(x):
    xf = x.astype(jnp.float32)
    mu = jnp.mean(xf, axis=-1, keepdims=True)
    var = jnp.mean(jnp.square(xf - mu), axis=-1, keepdims=True)
    return ((xf - mu) * lax.rsqrt(var + EPS)).astype(x.dtype)


def rms_norm(x, g):
    xf = x.astype(jnp.float32)
    return (xf * lax.rsqrt(jnp.mean(xf * xf, axis=-1, keepdims=True) + EPS)).astype(x.dtype) * g


def modulate(x, shift, scale):
    return layer_norm(x) * (1.0 + scale) + shift


def deepnorm_residual(x, y, g, b):
    return layer_norm(DN_ALPHA * x + y) * g + b


def axial_rope_table(n_rows, rot_dim):
    quarter = rot_dim // 4
    inv_freq = ROPE_BASE ** (-jnp.arange(quarter, dtype=jnp.float32) / quarter)
    row = jnp.repeat(jnp.arange(n_rows, dtype=jnp.float32), GRID_W)
    col = (jnp.arange(n_rows * GRID_W) % GRID_W).astype(jnp.float32)
    ang = jnp.stack([row[:, None] * inv_freq, col[:, None] * inv_freq], axis=1)
    return jnp.cos(ang), jnp.sin(ang)


def apply_rope(x, cos, sin):
    r, s = x.shape[-1], x.shape[1]
    bshape = (1, s) + (1,) * (x.ndim - 3) + (2, r // 4)
    c, sn = cos.reshape(bshape), sin.reshape(bshape)
    xf = x.astype(jnp.float32).reshape(x.shape[:-1] + (2, 2, r // 4))
    x1, x2 = xf[..., 0, :], xf[..., 1, :]
    out = jnp.stack([x1 * c - x2 * sn, x2 * c + x1 * sn], axis=-2)
    return out.reshape(x.shape).astype(x.dtype)


def multi_map_attention(q, k, v, coef):
    b, sq, h, m, dk = q.shape
    nb = sq // Q_BLOCK
    scale = dk ** -0.5
    qb = jnp.moveaxis(q.reshape(b, nb, Q_BLOCK, h, m, dk), 1, 0)

    def one_block(qblk):
        s = jnp.einsum('bqhmd,bkhmd->bhmqk', qblk, k).astype(jnp.float32) * scale
        w = jnp.einsum('bhmqk,m->bhqk', jax.nn.softmax(s, axis=-1), coef)
        return jnp.einsum('bhqk,bkhd->bqhd', w.astype(v.dtype), v)

    out = lax.map(one_block, qb)
    return jnp.moveaxis(out, 0, 1).reshape(b, sq, h, v.shape[-1])


def pool_mixer(u, w_pool, pool_scale):
    b, s, _ = u.shape
    ug = u.reshape(b, s, len(POOL_WINDOWS), POOL_GROUP)
    t = jnp.arange(s)
    outs = []
    for gi, w in enumerate(POOL_WINDOWS):
        half = w // 2
        xg = ug[:, :, gi, :].astype(jnp.float32)
        xp = jnp.pad(xg, ((0, 0), (half, half), (0, 0)))
        cs = jnp.concatenate([jnp.zeros((b, 1, POOL_GROUP), jnp.float32), jnp.cumsum(xp, axis=1)], axis=1)
        win = cs[:, w:w + s] - cs[:, :s]
        cnt = (jnp.minimum(t + half, s) - jnp.maximum(t - half, 0)).astype(jnp.float32)
        outs.append(win / cnt[None, :, None] - xg)
    pooled = jnp.stack(outs, axis=2).astype(u.dtype)
    y = jnp.einsum('bsgc,gcd->bsgd', pooled, w_pool)
    return y.reshape(b, s, POOL_W) * pool_scale


def token_mixer(u_lat, u_ctx, rope_mla, rope_diff, lam_init, ctx_out,
                w_in, b_gate, g_q, g_kv, w_uq, w_ukv, lam, g_sub, w_pool, pool_scale,
                w_br_mla, w_br_diff, w_br_pool, w_o):
    lam_val = (jnp.exp(jnp.sum(lam[0] * lam[1])) - jnp.exp(jnp.sum(lam[2] * lam[3]))).astype(jnp.float32) + lam_init
    coef_mla = jnp.ones((1,), jnp.float32)
    coef_diff = jnp.stack([jnp.ones((), jnp.float32), -lam_val])

    def keys_values(p, use_rope):
        b, s = p.shape[:2]
        c_kv, k_rot, k_diff, v_diff = split_cols(p[..., :KV_COLS], SPLIT_WIDTHS[:4])
        kv = (rms_norm(c_kv, g_kv) @ w_ukv).reshape(b, s, MLA_HEADS, MLA_NOPE + MLA_V)
        k_rot = k_rot.reshape(b, s, 1, MLA_ROPE)
        k_diff = k_diff.reshape(b, s, DIFF_HEADS, 2, DIFF_HD)
        if use_rope:
            k_rot = apply_rope(k_rot, *rope_mla)
            k_diff = apply_rope(k_diff, *rope_diff)
        k_mla = jnp.concatenate([kv[..., :MLA_NOPE], jnp.broadcast_to(k_rot, (b, s, MLA_HEADS, MLA_ROPE))], axis=-1)
        return (k_mla[:, :, :, None, :], kv[..., MLA_NOPE:], k_diff, v_diff.reshape(b, s, DIFF_HEADS, DIFF_V))

    def stream_out(p, use_rope, k_mla, v_mla, k_diff, v_diff):
        b, s = p.shape[:2]
        c_q, q_diff, pool_in, gate_in = split_cols(p[..., KV_COLS:], SPLIT_WIDTHS[4:])
        q = (rms_norm(c_q, g_q) @ w_uq).reshape(b, s, MLA_HEADS, MLA_NOPE + MLA_ROPE)
        q_nope, q_rot = q[..., :MLA_NOPE], q[..., MLA_NOPE:]
        q_diff = q_diff.reshape(b, s, DIFF_HEADS, 2, DIFF_HD)
        if use_rope:
            q_rot = apply_rope(q_rot, *rope_mla)
            q_diff = apply_rope(q_diff, *rope_diff)
        q_mla = jnp.concatenate([q_nope, q_rot], axis=-1)[:, :, :, None, :]
        o_mla = multi_map_attention(q_mla, k_mla, v_mla, coef_mla).reshape(b, s, MLA_HEADS * MLA_V)
        o_diff = multi_map_attention(q_diff, k_diff, v_diff, coef_diff)
        o_diff = (rms_norm(o_diff, g_sub) * (1.0 - lam_init)).reshape(b, s, DIFF_HEADS * DIFF_V)
        o_pool = pool_mixer(pool_in, w_pool, pool_scale)
        gates = jax.nn.sigmoid(gate_in.reshape(b, s, N_BRANCH, D_MODEL) + b_gate)
        merged = (gates[:, :, 0] * (o_mla @ w_br_mla)
                  + gates[:, :, 1] * (o_diff @ w_br_diff)
                  + gates[:, :, 2] * (o_pool @ w_br_pool))
        return merged @ w_o

    p_lat = u_lat @ w_in
    p_ctx = u_ctx @ (w_in if ctx_out else w_in[:, :KV_COLS])
    kv_lat = keys_values(p_lat, True)
    kv_ctx = keys_values(p_ctx, False)
    kv_all = tuple(jnp.concatenate([kc, kl], axis=1) for kc, kl in zip(kv_ctx, kv_lat))
    out_lat = stream_out(p_lat, True, *kv_all)
    out_ctx = stream_out(p_ctx, False, *kv_ctx) if ctx_out else None
    return out_lat, out_ctx


def expert_ffn(h, expert_idx, weights, w_gu, w_dn):
    n_tok, d = h.shape
    n_assign = n_tok * TOP_K
    flat_e = expert_idx.reshape(n_assign)
    order = jnp.argsort(flat_e)
    sorted_e = flat_e[order]
    sorted_tok = (order // TOP_K).astype(jnp.int32)
    sorted_w = weights.reshape(n_assign)[order]
    counts = jnp.bincount(flat_e, length=N_EXPERTS)
    padded = (counts + EXPERT_BLOCK - 1) // EXPERT_BLOCK * EXPERT_BLOCK
    pad_end = jnp.cumsum(padded)
    pad_start = pad_end - padded
    start = jnp.cumsum(counts) - counts
    dest = pad_start[sorted_e] + jnp.arange(n_assign) - start[sorted_e]
    n_blocks = -(-(n_assign + N_EXPERTS * (EXPERT_BLOCK - 1)) // EXPERT_BLOCK)
    n_rows = n_blocks * EXPERT_BLOCK
    slot_tok = jnp.full((n_rows,), n_tok, jnp.int32).at[dest].set(sorted_tok)
    slot_w = jnp.zeros((n_rows,), jnp.float32).at[dest].set(sorted_w)
    block_exp = jnp.minimum(jnp.searchsorted(pad_end, jnp.arange(n_blocks) * EXPERT_BLOCK, side='right'),
                            N_EXPERTS - 1)
    h_pad = jnp.concatenate([h, jnp.zeros((1, d), h.dtype)], axis=0)
    xb = h_pad[slot_tok].reshape(n_blocks, EXPERT_BLOCK, d)

    def run_block(args):
        xblk, e = args
        gate, up = jnp.split(xblk @ w_gu[e], 2, axis=-1)
        return (jax.nn.silu(gate) * up) @ w_dn[e]

    yb = lax.map(run_block, (xb, block_exp))
    y = yb.reshape(n_rows, d) * slot_w[:, None].astype(h.dtype)
    return jnp.zeros((n_tok + 1, d), h.dtype).at[slot_tok].add(y)[:n_tok]


def hierarchical_moe(h, w_grp, b_grp, w_exp, b_exp, w_gu, w_dn):
    n_tok = h.shape[0]
    grp_prob = jax.nn.softmax((h @ w_grp + b_grp).astype(jnp.float32), axis=-1)
    grp_p, grp_idx = lax.top_k(grp_prob, 1)
    exp_logits = (h @ w_exp + b_exp).astype(jnp.float32).reshape(n_tok, N_GROUPS, EXPERTS_PER_GROUP)
    in_grp = exp_logits[jnp.arange(n_tok), grp_idx[:, 0]]
    top_val, top_loc = lax.top_k(in_grp, TOP_K)
    weights = jax.nn.softmax(top_val, axis=-1) * grp_p
    expert_idx = grp_idx * EXPERTS_PER_GROUP + top_loc
    return expert_ffn(h, expert_idx, weights, w_gu, w_dn)


def setup_inputs(seed: int = 0) -> dict:
    key = jax.random.key(seed)
    ks = iter(jax.random.split(key, 40))
    D, L = D_MODEL, DEPTH

    def nrm(shape, scale):
        return jax.random.normal(next(ks), shape, jnp.float32) * scale

    return {
        "x": nrm((BATCH, SEQ, D), 1.0),
        "c": nrm((BATCH, D), 1.0),
        "ctx": nrm((BATCH, CTX_LEN, D), 1.0),
        "c_ctx": nrm((D,), 1.0),
        "w_ada": nrm((L, D, 6 * D), 0.5 * D ** -0.5),
        "b_ada": nrm((L, 6 * D), 0.01),
        "w_in": nrm((L, D, IN_COLS), D ** -0.5),
        "b_gate": nrm((L, N_BRANCH, D), 0.01),
        "g_q": 1.0 + nrm((L, MLA_Q_RANK), 0.01),
        "g_kv": 1.0 + nrm((L, MLA_KV_RANK), 0.01),
        "w_uq": nrm((L, MLA_Q_RANK, MLA_HEADS * (MLA_NOPE + MLA_ROPE)), MLA_Q_RANK ** -0.5),
        "w_ukv": nrm((L, MLA_KV_RANK, MLA_HEADS * (MLA_NOPE + MLA_V)), MLA_KV_RANK ** -0.5),
        "lam": nrm((L, 4, DIFF_HD), 0.1),
        "g_sub": 1.0 + nrm((L, DIFF_V), 0.01),
        "w_pool": nrm((L, len(POOL_WINDOWS), POOL_GROUP, POOL_GROUP), POOL_GROUP ** -0.5),
        "pool_scale": 1.0 + nrm((L, POOL_W), 0.01),
        "w_br_mla": nrm((L, MLA_HEADS * MLA_V, D), (MLA_HEADS * MLA_V) ** -0.5),
        "w_br_diff": nrm((L, DIFF_HEADS * DIFF_V, D), (DIFF_HEADS * DIFF_V) ** -0.5),
        "w_br_pool": nrm((L, POOL_W, D), POOL_W ** -0.5),
        "w_o": nrm((L, D, D), DN_BETA * D ** -0.5),
        "ln1_g": 1.0 + nrm((L, D), 0.01),
        "ln1_b": nrm((L, D), 0.01),
        "w_grp": nrm((L, D, N_GROUPS), D ** -0.5),
        "b_grp": nrm((L, N_GROUPS), 0.01),
        "w_exp": nrm((L, D, N_EXPERTS), D ** -0.5),
        "b_exp": nrm((L, N_EXPERTS), 0.01),
        "w_gu": nrm((L, N_EXPERTS, D, 2 * D_EXPERT), D ** -0.5),
        "w_dn": nrm((L, N_EXPERTS, D_EXPERT, D), DN_BETA * D_EXPERT ** -0.5),
        "ln2_g": 1.0 + nrm((L, D), 0.01),
        "ln2_b": nrm((L, D), 0.01),
    }


def reference(x, c, ctx, c_ctx, w_ada, b_ada, w_in, b_gate, g_q, g_kv, w_uq, w_ukv, lam, g_sub,
              w_pool, pool_scale, w_br_mla, w_br_diff, w_br_pool, w_o, ln1_g, ln1_b,
              w_grp, b_grp, w_exp, b_exp, w_gu, w_dn, ln2_g, ln2_b):
    n_lat = x.shape[1]
    rows = n_lat // GRID_W
    rope_mla = axial_rope_table(rows, MLA_ROPE)
    rope_diff = axial_rope_table(rows, DIFF_HD)
    x_lat, x_ctx = x, ctx
    for l in range(DEPTH):
        last = l == DEPTH - 1
        lam_init = 0.8 - 0.6 * math.exp(-0.3 * l)
        sh1, sc1, g1, sh2, sc2, g2 = jnp.split(jax.nn.silu(c) @ w_ada[l] + b_ada[l], 6, axis=-1)
        csh1, csc1, cg1, csh2, csc2, cg2 = jnp.split(jax.nn.silu(c_ctx) @ w_ada[l] + b_ada[l], 6, axis=-1)

        u_lat = modulate(x_lat, sh1[:, None], sc1[:, None])
        u_ctx = modulate(x_ctx, csh1, csc1)
        m_lat, m_ctx = token_mixer(u_lat, u_ctx, rope_mla, rope_diff, lam_init, not last,
                                   w_in[l], b_gate[l], g_q[l], g_kv[l], w_uq[l], w_ukv[l], lam[l], g_sub[l],
                                   w_pool[l], pool_scale[l], w_br_mla[l], w_br_diff[l], w_br_pool[l], w_o[l])
        x_lat = deepnorm_residual(x_lat, g1[:, None] * m_lat, ln1_g[l], ln1_b[l])
        if not last:
            x_ctx = deepnorm_residual(x_ctx, cg1 * m_ctx, ln1_g[l], ln1_b[l])

        b, s, d = x_lat.shape
        v_lat = modulate(x_lat, sh2[:, None], sc2[:, None]).reshape(b * s, d)
        if last:
            y_lat = hierarchical_moe(v_lat, w_grp[l], b_grp[l], w_exp[l], b_exp[l], w_gu[l], w_dn[l])
        else:
            v_ctx = modulate(x_ctx, csh2, csc2).reshape(-1, d)
            y = hierarchical_moe(jnp.concatenate([v_lat, v_ctx], axis=0),
                                 w_grp[l], b_grp[l], w_exp[l], b_exp[l], w_gu[l], w_dn[l])
            y_lat = y[:b * s]
            x_ctx = deepnorm_residual(x_ctx, cg2 * y[b * s:].reshape(x_ctx.shape), ln2_g[l], ln2_b[l])
        x_lat = deepnorm_residual(x_lat, g2[:, None] * y_lat.reshape(b, s, d), ln2_g[l], ln2_b[l])
    return x_lat
```

```python
import functools
import math

import jax
import jax.numpy as jnp
import numpy as np
from jax import lax
from jax.experimental import pallas as pl
from jax.experimental.pallas import tpu as pltpu

F32 = jnp.float32
BF16 = jnp.bfloat16

D = 2048
BATCH = 4
SEQ = 2048
CTX = 256
DEPTH = 2
GRID_W = 64
N_LAT = BATCH * SEQ
N_CTX = BATCH * CTX
N_ALL = N_LAT + N_CTX
MLA_HEADS = 8
MLA_Q_RANK = 768
MLA_KV_RANK = 512
MLA_NOPE = 128
MLA_ROPE = 64
MLA_V = 128
MLA_HEAD_PAD = 256
DIFF_HEADS = 4
DIFF_HD = 64
DIFF_V = 128
POOL_WINDOWS = (2, 4, 8, 16)
POOL_GROUP = 128
POOL_W = 512
N_GROUPS = 8
EXPERTS_PER_GROUP = 8
N_EXPERTS = 64
TOP_K = 2
D_EXPERT = 512
ROPE_BASE = 10000.0
EPS = 1e-6
DN_ALPHA = (2 * DEPTH) ** 0.25
MOD_ROWS = 8
CTX_MOD_ROW = BATCH

P_GATE = 0
P_CQ = 6144
P_KROT = 6912
P_CKV = 7168
P_KD = 7680
P_VD = 8192
P_QD = 8704
P_POOL = 9216
P_W = 9728
P_KV0 = 6656
P_KV_W = P_QD - P_KV0

LANE = 128
VMEM_LIMIT = 56 * 1024 * 1024

TM_PROJ = 1024
TN_PROJ = 512
TM_PREP = 512
TQ = 256
TM_MERGE = 256
TM_ROUTE = 512
TM_COMB = 256
EXPERT_ROWS = 256
ROUTE_LANES = 128


def _cparams(sem, vmem=VMEM_LIMIT):
    return pltpu.CompilerParams(dimension_semantics=sem, vmem_limit_bytes=vmem)


def _mod_index(row_block, rows_per_block):
    return jnp.minimum((row_block * rows_per_block) // SEQ, CTX_MOD_ROW)


def _layer_norm_rows(x):
    mu = jnp.mean(x, axis=-1, keepdims=True)
    xc = x - mu
    var = jnp.mean(xc * xc, axis=-1, keepdims=True)
    return xc * lax.rsqrt(var + EPS)


def _rms_rows(x):
    return x * lax.rsqrt(jnp.mean(x * x, axis=-1, keepdims=True) + EPS)


def _adaln_kernel(c_ref, w_ref, b_ref, o_ref):
    c = c_ref[...]
    s = (c * jax.nn.sigmoid(c)).astype(BF16)
    o_ref[...] = jnp.dot(s, w_ref[...].astype(BF16), preferred_element_type=F32) + b_ref[...]


def _adaln(cc, w_ada, b_ada):
    tn = 1024
    n = w_ada.shape[-1]
    return pl.pallas_call(
        _adaln_kernel,
        grid=(DEPTH, n // tn),
        in_specs=[pl.BlockSpec((MOD_ROWS, D), lambda l, j: (0, 0)),
                  pl.BlockSpec((None, D, tn), lambda l, j: (l, 0, j)),
                  pl.BlockSpec((None, 1, tn), lambda l, j: (l, 0, j))],
        out_specs=pl.BlockSpec((None, MOD_ROWS, tn), lambda l, j: (l, 0, j)),
        out_shape=jax.ShapeDtypeStruct((DEPTH, MOD_ROWS, n), F32),
        compiler_params=_cparams(("arbitrary", "arbitrary")),
        name="adaln",
    )(cc, w_ada, b_ada.reshape(DEPTH, 1, n))


def _proj_kernel(x_ref, sh_ref, sc_ref, w_ref, o_ref, u_ref):
    @pl.when(pl.program_id(1) == 0)
    def _():
        u = _layer_norm_rows(x_ref[...]) * (1.0 + sc_ref[0]) + sh_ref[0]
        u_ref[...] = u.astype(BF16)

    o_ref[...] = jnp.dot(u_ref[...], w_ref[...], preferred_element_type=F32).astype(o_ref.dtype)


def _proj(x_all, mods, w, row0, nrows, col0, ncols):
    tm = min(TM_PROJ, nrows)
    rb0, cb0 = row0 // tm, col0 // TN_PROJ
    return pl.pallas_call(
        _proj_kernel,
        grid=(nrows // tm, ncols // TN_PROJ),
        in_specs=[pl.BlockSpec((tm, D), lambda i, j: (rb0 + i, 0)),
                  pl.BlockSpec((1, 1, D), lambda i, j: (_mod_index(rb0 + i, tm), 0, 0)),
                  pl.BlockSpec((1, 1, D), lambda i, j: (_mod_index(rb0 + i, tm), 0, 1)),
                  pl.BlockSpec((D, TN_PROJ), lambda i, j: (0, cb0 + j))],
        out_specs=pl.BlockSpec((tm, TN_PROJ), lambda i, j: (i, j)),
        out_shape=jax.ShapeDtypeStruct((nrows, ncols), BF16),
        scratch_shapes=[pltpu.VMEM((tm, D), BF16)],
        compiler_params=_cparams(("arbitrary", "arbitrary")),
        name="proj_in",
    )(x_all, mods, mods, w)


def _rope128(x, cos, sin):
    lane = lax.broadcasted_iota(jnp.int32, x.shape, 1)
    nxt = pltpu.roll(x, LANE - MLA_ROPE // 4, 1)
    prv = pltpu.roll(x, MLA_ROPE // 4, 1)
    partner = jnp.where((lane % (MLA_ROPE // 2)) < MLA_ROPE // 4, nxt, prv)
    return x * cos + partner * sin


def _kvprep_kernel(ckv_ref, kd_ref, kr_ref, cos_ref, sin_ref, g_ref, w_ref, k_ref, v_ref, kdo_ref):
    cos, sin = cos_ref[...], sin_ref[...]
    c = (_rms_rows(ckv_ref[...].astype(F32)) * g_ref[...]).astype(BF16)
    kv = jnp.dot(c, w_ref[...], preferred_element_type=F32)
    krot = _rope128(kr_ref[...].astype(F32), cos, sin).astype(BF16)
    for h in range(MLA_HEADS):
        k_ref[:, h * MLA_HEAD_PAD:h * MLA_HEAD_PAD + MLA_NOPE] = kv[:, h * MLA_NOPE:(h + 1) * MLA_NOPE].astype(BF16)
        k_ref[:, h * MLA_HEAD_PAD + MLA_NOPE:(h + 1) * MLA_HEAD_PAD] = krot
    v_ref[...] = kv[:, MLA_HEADS * MLA_NOPE:].astype(BF16)
    for h in range(DIFF_HEADS):
        sl = slice(h * LANE, (h + 1) * LANE)
        kdo_ref[:, sl] = _rope128(kd_ref[:, sl].astype(F32), cos, sin).astype(BF16)


def _rope_block_index(i, tm, row0):
    r = row0 + i * tm
    return jnp.where(r < N_LAT, (r % SEQ) // tm, SEQ // tm)


def _kvprep(p, pcol0, row0, nrows, rope_cos, rope_sin, g_kv, w_ukv):
    tm = min(TM_PREP, nrows)
    rb0 = row0 // tm
    return pl.pallas_call(
        _kvprep_kernel,
        grid=(nrows // tm,),
        in_specs=[pl.BlockSpec((tm, MLA_KV_RANK), lambda i: (i, (P_CKV - pcol0) // MLA_KV_RANK)),
                  pl.BlockSpec((tm, 512), lambda i: (i, (P_KD - pcol0) // 512)),
                  pl.BlockSpec((tm, LANE), lambda i: (i, (P_KROT - pcol0) // LANE)),
                  pl.BlockSpec((tm, LANE), lambda i: (_rope_block_index(i, tm, row0), 0)),
                  pl.BlockSpec((tm, LANE), lambda i: (_rope_block_index(i, tm, row0), 0)),
                  pl.BlockSpec((1, MLA_KV_RANK), lambda i: (0, 0)),
                  pl.BlockSpec((MLA_KV_RANK, 2 * MLA_HEADS * MLA_NOPE), lambda i: (0, 0))],
        out_specs=[pl.BlockSpec((tm, MLA_HEADS * MLA_HEAD_PAD), lambda i: (i, 0)),
                   pl.BlockSpec((tm, MLA_HEADS * MLA_V), lambda i: (i, 0)),
                   pl.BlockSpec((tm, 512), lambda i: (i, 0))],
        out_shape=[jax.ShapeDtypeStruct((nrows, MLA_HEADS * MLA_HEAD_PAD), BF16),
                   jax.ShapeDtypeStruct((nrows, MLA_HEADS * MLA_V), BF16),
                   jax.ShapeDtypeStruct((nrows, 512), BF16)],
        compiler_params=_cparams(("arbitrary",)),
        name="kv_prep",
    )(p, p, p, rope_cos, rope_sin, g_kv, w_ukv)


def _qprep_kernel(cq_ref, qd_ref, cos_ref, sin_ref, g_ref, w_ref, q_ref, qdo_ref):
    cos, sin = cos_ref[...], sin_ref[...]
    c = (_rms_rows(cq_ref[...].astype(F32)) * g_ref[...]).astype(BF16)
    q = jnp.dot(c, w_ref[...], preferred_element_type=F32)
    scale = (MLA_NOPE + MLA_ROPE) ** -0.5
    for h in range(MLA_HEADS):
        lo = h * MLA_HEAD_PAD
        q_ref[:, lo:lo + MLA_NOPE] = (q[:, lo:lo + MLA_NOPE] * scale).astype(BF16)
        q_ref[:, lo + MLA_NOPE:lo + MLA_HEAD_PAD] = (
            _rope128(q[:, lo + MLA_NOPE:lo + MLA_HEAD_PAD], cos, sin) * scale).astype(BF16)
    dscale = DIFF_HD ** -0.5
    for h in range(DIFF_HEADS):
        sl = slice(h * LANE, (h + 1) * LANE)
        qdo_ref[:, sl] = (_rope128(qd_ref[:, sl].astype(F32), cos, sin) * dscale).astype(BF16)


def _qprep(p, row0, nrows, rope_cos, rope_sin, g_q, w_uq):
    tm = min(TM_PREP, nrows)
    return pl.pallas_call(
        _qprep_kernel,
        grid=(nrows // tm,),
        in_specs=[pl.BlockSpec((tm, MLA_Q_RANK), lambda i: (i, P_CQ // MLA_Q_RANK)),
                  pl.BlockSpec((tm, 512), lambda i: (i, P_QD // 512)),
                  pl.BlockSpec((tm, LANE), lambda i: (_rope_block_index(i, tm, row0), 0)),
                  pl.BlockSpec((tm, LANE), lambda i: (_rope_block_index(i, tm, row0), 0)),
                  pl.BlockSpec((1, MLA_Q_RANK), lambda i: (0, 0)),
                  pl.BlockSpec((MLA_Q_RANK, MLA_HEADS * MLA_HEAD_PAD), lambda i: (0, 0))],
        out_specs=[pl.BlockSpec((tm, MLA_HEADS * MLA_HEAD_PAD), lambda i: (i, 0)),
                   pl.BlockSpec((tm, 512), lambda i: (i, 0))],
        out_shape=[jax.ShapeDtypeStruct((nrows, MLA_HEADS * MLA_HEAD_PAD), BF16),
                   jax.ShapeDtypeStruct((nrows, 512), BF16)],
        compiler_params=_cparams(("arbitrary",)),
        name="q_prep",
    )(p, p, rope_cos, rope_sin, g_q, w_uq)


_NT = (((1,), (1,)), ((), ()))


def _softmax_parts(scores):
    m = scores[0].max(axis=-1, keepdims=True)
    for s in scores[1:]:
        m = jnp.maximum(m, s.max(axis=-1, keepdims=True))
    ps = [jnp.exp(s - m) for s in scores]
    tot = ps[0].sum(axis=-1, keepdims=True)
    for p in ps[1:]:
        tot = tot + p.sum(axis=-1, keepdims=True)
    return ps, 1.0 / tot


def _mla_attn_kernel(n_seg, q_ref, *refs):
    k_refs, v_refs, o_ref = refs[:n_seg], refs[n_seg:2 * n_seg], refs[2 * n_seg]
    q = q_ref[...]
    scores = [lax.dot_general(q, k[...], _NT, preferred_element_type=F32) for k in k_refs]
    ps, inv = _softmax_parts(scores)
    acc = jnp.dot(ps[0].astype(BF16), v_refs[0][...], preferred_element_type=F32)
    for p, v in zip(ps[1:], v_refs[1:]):
        acc = acc + jnp.dot(p.astype(BF16), v[...], preferred_element_type=F32)
    o_ref[...] = (acc * inv).astype(o_ref.dtype)


def _diff_attn_kernel(n_seg, lam_init, q_ref, lam_ref, g_ref, *refs):
    k_refs, v_refs, o_ref = refs[:n_seg], refs[n_seg:2 * n_seg], refs[2 * n_seg]
    q = q_ref[...]
    lane = lax.broadcasted_iota(jnp.int32, q.shape, 1)
    q1 = jnp.where(lane < DIFF_HD, q, jnp.zeros_like(q))
    q2 = jnp.where(lane < DIFF_HD, jnp.zeros_like(q), q)
    lam = lam_ref[...]
    lam_val = (jnp.exp(jnp.sum(lam[0:1] * lam[1:2], axis=-1, keepdims=True))
               - jnp.exp(jnp.sum(lam[2:3] * lam[3:4], axis=-1, keepdims=True))) + lam_init
    s1 = [lax.dot_general(q1, k[...], _NT, preferred_element_type=F32) for k in k_refs]
    s2 = [lax.dot_general(q2, k[...], _NT, preferred_element_type=F32) for k in k_refs]
    p1, inv1 = _softmax_parts(s1)
    p2, inv2 = _softmax_parts(s2)
    c2 = lam_val * inv2
    acc = None
    for a, b, v in zip(p1, p2, v_refs):
        w = (a * inv1 - b * c2).astype(BF16)
        t = jnp.dot(w, v[...], preferred_element_type=F32)
        acc = t if acc is None else acc + t
    o_ref[...] = (_rms_rows(acc) * g_ref[...] * (1.0 - lam_init)).astype(o_ref.dtype)


def _attention(kind, q, q_col0, q_row0, n_q_per_batch, kv_segs, lam_init=None, lam=None, g_sub=None):
    heads, dq = (MLA_HEADS, MLA_HEAD_PAD) if kind == "mla" else (DIFF_HEADS, LANE)
    tq = min(TQ, n_q_per_batch)
    nq = n_q_per_batch // tq
    qb0 = q_row0 // tq
    qcb = q_col0 // dq
    n_seg = len(kv_segs)
    in_specs = [pl.BlockSpec((tq, dq), lambda b, h, i: (qb0 + b * nq + i, qcb + h))]
    args = [q]
    if kind == "diff":
        in_specs += [pl.BlockSpec((4, DIFF_HD), lambda b, h, i: (0, 0)),
                     pl.BlockSpec((1, DIFF_V), lambda b, h, i: (0, 0))]
        args += [lam, g_sub]
    for (k_arr, kcb, _, _, row0, rpb) in kv_segs:
        in_specs.append(pl.BlockSpec((rpb, dq), functools.partial(
            lambda b, h, i, r0, c0: (r0 + b, c0 + h), r0=row0 // rpb, c0=kcb)))
        args.append(k_arr)
    for (_, _, v_arr, vcb, row0, rpb) in kv_segs:
        in_specs.append(pl.BlockSpec((rpb, LANE), functools.partial(
            lambda b, h, i, r0, c0: (r0 + b, c0 + h), r0=row0 // rpb, c0=vcb)))
        args.append(v_arr)
    if kind == "mla":
        body = functools.partial(_mla_attn_kernel, n_seg)
    else:
        body = functools.partial(_diff_attn_kernel, n_seg, lam_init)
    return pl.pallas_call(
        body,
        grid=(BATCH, heads, nq),
        in_specs=in_specs,
        out_specs=pl.BlockSpec((tq, LANE), lambda b, h, i: (b * nq + i, h)),
        out_shape=jax.ShapeDtypeStruct((BATCH * n_q_per_batch, heads * LANE), BF16),
        compiler_params=_cparams(("arbitrary", "arbitrary", "arbitrary")),
        name=kind + "_attn",
    )(*args)


def _pool_kernel(seq, x_ref, w_ref, s_ref, o_ref):
    row = lax.broadcasted_iota(jnp.int32, (seq, POOL_GROUP), 0)

    def down(x, k):
        return jnp.where(row >= k, pltpu.roll(x, k, 0), 0.0)

    def up(x, k):
        return jnp.where(row < seq - k, pltpu.roll(x, seq - k, 0), 0.0)

    t = row.astype(F32)
    for gi, w in enumerate(POOL_WINDOWS):
        half = w // 2
        sl = slice(gi * POOL_GROUP, (gi + 1) * POOL_GROUP)
        x = x_ref[:, sl].astype(F32)
        before, after, span = down(x, 1), x, 1
        while span < half:
            before = before + down(before, span)
            after = after + up(after, span)
            span *= 2
        cnt = jnp.minimum(t + half, float(seq)) - jnp.maximum(t - half, 0.0)
        pooled = ((before + after) / cnt - x).astype(BF16)
        y = jnp.dot(pooled, w_ref[gi], preferred_element_type=F32)
        o_ref[:, sl] = (y * s_ref[:, sl]).astype(o_ref.dtype)


def _pool(p, row0, seq, w_pool, pool_scale):
    rb0 = row0 // seq
    return pl.pallas_call(
        functools.partial(_pool_kernel, seq),
        grid=(BATCH,),
        in_specs=[pl.BlockSpec((seq, POOL_W), lambda b: (rb0 + b, P_POOL // POOL_W)),
                  pl.BlockSpec((len(POOL_WINDOWS), POOL_GROUP, POOL_GROUP), lambda b: (0, 0, 0)),
                  pl.BlockSpec((1, POOL_W), lambda b: (0, 0))],
        out_specs=pl.BlockSpec((seq, POOL_W), lambda b: (b, 0)),
        out_shape=jax.ShapeDtypeStruct((BATCH * seq, POOL_W), BF16),
        compiler_params=_cparams(("arbitrary",)),
        name="pool_mix",
    )(p, w_pool, pool_scale)


def _deepnorm(x, y, gate, g, b):
    return _layer_norm_rows(DN_ALPHA * x + gate * y) * g + b


def _merge_kernel(x_ref, gt_ref, om_ref, od_ref, op_ref, bg_ref, wm_ref, wd_ref, wp_ref, wo_ref,
                  g1_ref, lg_ref, lb_ref, o_ref):
    merged = None
    for br, (o_br, w_br) in enumerate(((om_ref, wm_ref), (od_ref, wd_ref), (op_ref, wp_ref))):
        y = jnp.dot(o_br[...], w_br[...], preferred_element_type=F32)
        gate = jax.nn.sigmoid(gt_ref[:, br * D:(br + 1) * D].astype(F32) + bg_ref[br:br + 1, :])
        merged = gate * y if merged is None else merged + gate * y
    out = jnp.dot(merged.astype(BF16), wo_ref[...], preferred_element_type=F32)
    o_ref[...] = _deepnorm(x_ref[...], out, g1_ref[0], lg_ref[...], lb_ref[...])


def _merge(x_all, p, o_mla, o_diff, o_pool, mods, b_gate, w_br_mla, w_br_diff, w_br_pool, w_o, ln_g, ln_b, nrows):
    tm = TM_MERGE
    const = lambda i: (0, 0)
    once = dict(pipeline_mode=pl.Buffered(1))
    return pl.pallas_call(
        _merge_kernel,
        grid=(nrows // tm,),
        in_specs=[pl.BlockSpec((tm, D), lambda i: (i, 0)),
                  pl.BlockSpec((tm, 3 * D), lambda i: (i, P_GATE // (3 * D))),
                  pl.BlockSpec((tm, MLA_HEADS * MLA_V), lambda i: (i, 0)),
                  pl.BlockSpec((tm, DIFF_HEADS * DIFF_V), lambda i: (i, 0)),
                  pl.BlockSpec((tm, POOL_W), lambda i: (i, 0)),
                  pl.BlockSpec((3, D), const, **once),
                  pl.BlockSpec((MLA_HEADS * MLA_V, D), const, **once),
                  pl.BlockSpec((DIFF_HEADS * DIFF_V, D), const, **once),
                  pl.BlockSpec((POOL_W, D), const, **once),
                  pl.BlockSpec((D, D), const, **once),
                  pl.BlockSpec((1, 1, D), lambda i: (_mod_index(i, tm), 0, 2)),
                  pl.BlockSpec((1, D), const, **once),
                  pl.BlockSpec((1, D), const, **once)],
        out_specs=pl.BlockSpec((tm, D), lambda i: (i, 0)),
        out_shape=jax.ShapeDtypeStruct((nrows, D), F32),
        compiler_params=_cparams(("arbitrary",)),
        name="merge_out",
    )(x_all, p, o_mla, o_diff, o_pool, b_gate, w_br_mla, w_br_diff, w_br_pool, w_o, mods, ln_g, ln_b)


def _route_kernel(x_ref, sh_ref, sc_ref, w_ref, b_ref, v_ref, r_ref):
    v = _layer_norm_rows(x_ref[...]) * (1.0 + sc_ref[0]) + sh_ref[0]
    v_ref[...] = v
    v_hi = v.astype(BF16)
    v_lo = (v - v_hi.astype(F32)).astype(BF16)
    t = jnp.dot(v_hi, w_ref[...], preferred_element_type=F32)
    logit = (t[:, :ROUTE_LANES] + t[:, ROUTE_LANES:]
             + jnp.dot(v_lo, w_ref[:, :ROUTE_LANES], preferred_element_type=F32) + b_ref[...])
    lane = lax.broadcasted_iota(jnp.int32, logit.shape, 1).astype(F32)
    neg = jnp.float32(-jnp.inf)
    big = jnp.float32(ROUTE_LANES)
    gl = jnp.where(lane < N_GROUPS, logit, neg)
    gmax = gl.max(axis=-1, keepdims=True)
    grp_p = 1.0 / jnp.exp(gl - gmax).sum(axis=-1, keepdims=True)
    grp = jnp.where(gl == gmax, lane, big).min(axis=-1, keepdims=True)
    grp_lo = N_GROUPS + EXPERTS_PER_GROUP * grp
    in_grp = (lane >= grp_lo) & (lane < grp_lo + EXPERTS_PER_GROUP)
    el = jnp.where(in_grp, logit, neg)
    t1 = el.max(axis=-1, keepdims=True)
    i1 = jnp.where(el == t1, lane, big).min(axis=-1, keepdims=True)
    el2 = jnp.where(lane == i1, neg, el)
    t2 = el2.max(axis=-1, keepdims=True)
    i2 = jnp.where(el2 == t2, lane, big).min(axis=-1, keepdims=True)
    e = jnp.exp(t2 - t1)
    w1 = grp_p / (1.0 + e)
    w2 = w1 * e
    r_ref[...] = jnp.where(lane == 0, i1 - N_GROUPS,
                           jnp.where(lane == 1, i2 - N_GROUPS,
                                     jnp.where(lane == 2, w1, jnp.where(lane == 3, w2, 0.0))))


def _route(x_all, mods, w_route, b_route, nrows):
    tm = TM_ROUTE
    return pl.pallas_call(
        _route_kernel,
        grid=(nrows // tm,),
        in_specs=[pl.BlockSpec((tm, D), lambda i: (i, 0)),
                  pl.BlockSpec((1, 1, D), lambda i: (_mod_index(i, tm), 0, 3)),
                  pl.BlockSpec((1, 1, D), lambda i: (_mod_index(i, tm), 0, 4)),
                  pl.BlockSpec((D, 2 * ROUTE_LANES), lambda i: (0, 0)),
                  pl.BlockSpec((1, ROUTE_LANES), lambda i: (0, 0))],
        out_specs=[pl.BlockSpec((tm, D), lambda i: (i, 0)),
                   pl.BlockSpec((tm, ROUTE_LANES), lambda i: (i, 0))],
        out_shape=[jax.ShapeDtypeStruct((nrows, D), F32),
                   jax.ShapeDtypeStruct((nrows, ROUTE_LANES), F32)],
        compiler_params=_cparams(("arbitrary",)),
        name="moe_route",
    )(x_all, mods, mods, w_route, b_route)


def _gather_rows(src_hbm, idx_ref, base, n, dst, sem):
    def body(r, carry):
        tok = idx_ref[base + r]
        pltpu.make_async_copy(src_hbm.at[pl.ds(tok, 1), :], dst.at[pl.ds(r, 1), :], sem).start()
        return carry
    lax.fori_loop(0, n, body, 0)


def _wait_rows(src_hbm, n, dst, sem):
    pltpu.make_async_copy(src_hbm.at[pl.ds(0, n), :], dst, sem).wait()


def _ffn_kernel(bexp_ref, slot_ref, nused_ref, v_hbm, wgu_ref, wdn_ref, o_ref, xbuf, wgu_bf, wdn_bf, sems):
    i = pl.program_id(0)
    n_used = nused_ref[0]
    cur = i % 2

    @pl.when(i == 0)
    def _():
        _gather_rows(v_hbm, slot_ref, 0, EXPERT_ROWS, xbuf.at[0], sems.at[0])

    @pl.when(i + 1 < n_used)
    def _():
        _gather_rows(v_hbm, slot_ref, (i + 1) * EXPERT_ROWS, EXPERT_ROWS, xbuf.at[1 - cur], sems.at[1 - cur])

    @pl.when((i == 0) | (bexp_ref[i] != bexp_ref[jnp.maximum(i - 1, 0)]))
    def _():
        wgu_bf[...] = wgu_ref[...].astype(BF16)
        wdn_bf[...] = wdn_ref[...].astype(BF16)

    @pl.when((i < n_used) | (i == 0))
    def _():
        _wait_rows(v_hbm, EXPERT_ROWS, xbuf.at[cur], sems.at[cur])

    @pl.when(i < n_used)
    def _():
        x = xbuf[cur].astype(BF16)
        h = jnp.dot(x, wgu_bf[...], preferred_element_type=F32)
        gate, up = h[:, :D_EXPERT], h[:, D_EXPERT:]
        a = (gate * jax.nn.sigmoid(gate) * up).astype(BF16)
        o_ref[...] = jnp.dot(a, wdn_bf[...], preferred_element_type=F32)

    @pl.when(i >= n_used)
    def _():
        o_ref[...] = jnp.zeros_like(o_ref)


def _expert_ffn(v, block_exp, slot_tok, n_used, w_gu, w_dn):
    n_blocks = block_exp.shape[0]
    grid_spec = pltpu.PrefetchScalarGridSpec(
        num_scalar_prefetch=3,
        grid=(n_blocks,),
        in_specs=[pl.BlockSpec(memory_space=pl.ANY),
                  pl.BlockSpec((None, D, 2 * D_EXPERT), lambda i, be, st, nu: (be[i], 0, 0)),
                  pl.BlockSpec((None, D_EXPERT, D), lambda i, be, st, nu: (be[i], 0, 0))],
        out_specs=pl.BlockSpec((EXPERT_ROWS, D), lambda i, be, st, nu: (i, 0)),
        scratch_shapes=[pltpu.VMEM((2, EXPERT_ROWS, D), F32),
                        pltpu.VMEM((D, 2 * D_EXPERT), BF16),
                        pltpu.VMEM((D_EXPERT, D), BF16),
                        pltpu.SemaphoreType.DMA((2,))],
    )
    return pl.pallas_call(
        _ffn_kernel,
        grid_spec=grid_spec,
        out_shape=jax.ShapeDtypeStruct((n_blocks * EXPERT_ROWS, D), F32),
        compiler_params=_cparams(("arbitrary",)),
        name="expert_ffn",
    )(block_exp, slot_tok, n_used, v, w_gu, w_dn)


def _combine_kernel(dest_ref, y_hbm, x_ref, r_ref, g2_ref, lg_ref, lb_ref, o_ref, ybuf, sems):
    i = pl.program_id(0)
    n = pl.num_programs(0)
    tm = x_ref.shape[0]
    cur = i % 2

    @pl.when(i == 0)
    def _():
        _gather_rows(y_hbm, dest_ref, 0, TOP_K * tm, ybuf.at[0], sems.at[0])

    @pl.when(i + 1 < n)
    def _():
        _gather_rows(y_hbm, dest_ref, (i + 1) * TOP_K * tm, TOP_K * tm, ybuf.at[1 - cur], sems.at[1 - cur])

    _wait_rows(y_hbm, TOP_K * tm, ybuf.at[cur], sems.at[cur])
    r = r_ref[...]
    y = ybuf[cur, 0:tm, :] * r[:, 2:3] + ybuf[cur, tm:2 * tm, :] * r[:, 3:4]
    o_ref[...] = _deepnorm(x_ref[...], y, g2_ref[0], lg_ref[...], lb_ref[...])


def _combine(x_all, yb, dest, route, mods, ln_g, ln_b, nrows):
    tm = TM_COMB
    grid_spec = pltpu.PrefetchScalarGridSpec(
        num_scalar_prefetch=1,
        grid=(nrows // tm,),
        in_specs=[pl.BlockSpec(memory_space=pl.ANY),
                  pl.BlockSpec((tm, D), lambda i, d: (i, 0)),
                  pl.BlockSpec((tm, ROUTE_LANES), lambda i, d: (i, 0)),
                  pl.BlockSpec((1, 1, D), lambda i, d: (_mod_index(i, tm), 0, 5)),
                  pl.BlockSpec((1, D), lambda i, d: (0, 0)),
                  pl.BlockSpec((1, D), lambda i, d: (0, 0))],
        out_specs=pl.BlockSpec((tm, D), lambda i, d: (i, 0)),
        scratch_shapes=[pltpu.VMEM((2, TOP_K * tm, D), F32),
                        pltpu.SemaphoreType.DMA((2,))],
    )
    return pl.pallas_call(
        _combine_kernel,
        grid_spec=grid_spec,
        out_shape=jax.ShapeDtypeStruct((nrows, D), F32),
        compiler_params=_cparams(("arbitrary",)),
        name="moe_combine",
    )(dest, yb, x_all, route, mods, ln_g, ln_b)


def _dispatch_indices(route, n_tok):
    n_assign = n_tok * TOP_K
    flat_e = route[:, :TOP_K].astype(jnp.int32).reshape(n_assign)
    order = jnp.argsort(flat_e)
    sorted_e = flat_e[order]
    counts = jnp.bincount(flat_e, length=N_EXPERTS)
    padded = (counts + EXPERT_ROWS - 1) // EXPERT_ROWS * EXPERT_ROWS
    pad_end = jnp.cumsum(padded)
    pad_start = pad_end - padded
    start = jnp.cumsum(counts) - counts
    dest_sorted = (pad_start[sorted_e] + jnp.arange(n_assign) - start[sorted_e]).astype(jnp.int32)
    n_blocks = -(-(n_assign + N_EXPERTS * (EXPERT_ROWS - 1)) // EXPERT_ROWS)
    slot_tok = jnp.zeros((n_blocks * EXPERT_ROWS,), jnp.int32).at[dest_sorted].set((order // TOP_K).astype(jnp.int32))
    dest = jnp.zeros((n_assign,), jnp.int32).at[order].set(dest_sorted)
    n_used = (pad_end[-1] // EXPERT_ROWS).astype(jnp.int32)
    blk = jnp.arange(n_blocks, dtype=jnp.int32)
    block_exp = jnp.minimum(jnp.searchsorted(pad_end, blk * EXPERT_ROWS, side="right"), N_EXPERTS - 1).astype(jnp.int32)
    block_exp = jnp.where(blk < n_used, block_exp, block_exp[jnp.maximum(n_used - 1, 0)])
    dest_blocks = dest.reshape(n_tok // TM_COMB, TM_COMB, TOP_K).transpose(0, 2, 1).reshape(n_assign)
    return block_exp, slot_tok, n_used.reshape(1), dest_blocks


def _split_in_cols(w):
    offs = np.cumsum((MLA_KV_RANK, MLA_ROPE, 512, 512, MLA_Q_RANK, 512, POOL_W))
    return jnp.split(w, offs.tolist(), axis=-1)


def _relayout_w_in(w_in):
    c_kv, k_rot, k_diff, v_diff, c_q, q_diff, pool_in, gate = _split_in_cols(w_in)
    zeros = jnp.zeros(w_in.shape[:-1] + (P_CKV - P_KROT - MLA_ROPE,), w_in.dtype)
    return jnp.concatenate([gate, c_q, k_rot, zeros, c_kv, k_diff, v_diff, q_diff, pool_in], axis=-1).astype(BF16)


def _relayout_w_uq(w_uq):
    w = w_uq.reshape(DEPTH, MLA_Q_RANK, MLA_HEADS, MLA_NOPE + MLA_ROPE)
    w = jnp.pad(w, ((0, 0), (0, 0), (0, 0), (0, MLA_HEAD_PAD - MLA_NOPE - MLA_ROPE)))
    return w.reshape(DEPTH, MLA_Q_RANK, MLA_HEADS * MLA_HEAD_PAD).astype(BF16)


def _relayout_w_ukv(w_ukv):
    w = w_ukv.reshape(DEPTH, MLA_KV_RANK, MLA_HEADS, MLA_NOPE + MLA_V)
    k = w[..., :MLA_NOPE].reshape(DEPTH, MLA_KV_RANK, MLA_HEADS * MLA_NOPE)
    v = w[..., MLA_NOPE:].reshape(DEPTH, MLA_KV_RANK, MLA_HEADS * MLA_V)
    return jnp.concatenate([k, v], axis=-1).astype(BF16)


def _relayout_router(w_grp, b_grp, w_exp, b_exp):
    w = jnp.concatenate([w_grp, w_exp], axis=-1)
    w = jnp.pad(w, ((0, 0), (0, 0), (0, ROUTE_LANES - w.shape[-1])))
    w_hi = w.astype(BF16)
    w_lo = (w - w_hi.astype(F32)).astype(BF16)
    b = jnp.pad(jnp.concatenate([b_grp, b_exp], axis=-1), ((0, 0), (0, ROUTE_LANES - N_GROUPS - N_EXPERTS)))
    return jnp.concatenate([w_hi, w_lo], axis=-1), b.reshape(DEPTH, 1, ROUTE_LANES)


def _rope_tables():
    quarter = MLA_ROPE // 4
    inv_freq = ROPE_BASE ** (-jnp.arange(quarter, dtype=F32) / quarter)
    pos = jnp.arange(SEQ)
    row = (pos // GRID_W).astype(F32)[:, None] * inv_freq
    col = (pos % GRID_W).astype(F32)[:, None] * inv_freq
    cos64 = jnp.concatenate([jnp.cos(row), jnp.cos(row), jnp.cos(col), jnp.cos(col)], axis=-1)
    sin64 = jnp.concatenate([-jnp.sin(row), jnp.sin(row), -jnp.sin(col), jnp.sin(col)], axis=-1)
    cos = jnp.concatenate([jnp.tile(cos64, (1, 2)), jnp.ones((TM_PREP, LANE), F32)], axis=0)
    sin = jnp.concatenate([jnp.tile(sin64, (1, 2)), jnp.zeros((TM_PREP, LANE), F32)], axis=0)
    return cos, sin


def kernel(x, c, ctx, c_ctx, w_ada, b_ada, w_in, b_gate, g_q, g_kv, w_uq, w_ukv, lam, g_sub, w_pool, pool_scale,
           w_br_mla, w_br_diff, w_br_pool, w_o, ln1_g, ln1_b, w_grp, b_grp, w_exp, b_exp, w_gu, w_dn, ln2_g, ln2_b):
    assert x.shape == (BATCH, SEQ, D) and ctx.shape == (BATCH, CTX, D)
    assert MLA_ROPE == DIFF_HD
    cc = jnp.concatenate([c, c_ctx[None], jnp.zeros((MOD_ROWS - BATCH - 1, D), F32)], axis=0)
    ada = _adaln(cc, w_ada, b_ada)
    w_in_p = _relayout_w_in(w_in)
    w_uq_p = _relayout_w_uq(w_uq)
    w_ukv_p = _relayout_w_ukv(w_ukv)
    w_route, b_route = _relayout_router(w_grp, b_grp, w_exp, b_exp)
    rope_cos, rope_sin = _rope_tables()
    bf = lambda a: a.astype(BF16)
    w_pool_b, w_br_mla_b, w_br_diff_b, w_br_pool_b, w_o_b = map(bf, (w_pool, w_br_mla, w_br_diff, w_br_pool, w_o))

    x_all = jnp.concatenate([x.reshape(N_LAT, D), ctx.reshape(N_CTX, D)], axis=0)
    for l in range(DEPTH):
        last = l == DEPTH - 1
        lam_init = 0.8 - 0.6 * math.exp(-0.3 * l)
        mods = ada[l].reshape(MOD_ROWS, 1, 6 * D)
        n_out = N_LAT if last else N_ALL
        g_kv_l, g_q_l = g_kv[l].reshape(1, -1), g_q[l].reshape(1, -1)
        g_sub_l, ps_l = g_sub[l].reshape(1, -1), pool_scale[l].reshape(1, -1)

        p = _proj(x_all, mods, w_in_p[l], 0, n_out, 0, P_W)
        k_mla, v_mla, k_diff = _kvprep(p, 0, 0, n_out, rope_cos, rope_sin, g_kv_l, w_ukv_p[l])
        q_mla, q_diff = _qprep(p, 0, n_out, rope_cos, rope_sin, g_q_l, w_uq_p[l])
        if last:
            p_c = _proj(x_all, mods, w_in_p[l], N_LAT, N_CTX, P_KV0, P_KV_W)
            k_mla_c, v_mla_c, k_diff_c = _kvprep(p_c, P_KV0, N_LAT, N_CTX, rope_cos, rope_sin, g_kv_l, w_ukv_p[l])
            c_row0, pv_c, pv_c_col = 0, p_c, (P_VD - P_KV0) // LANE
        else:
            k_mla_c, v_mla_c, k_diff_c, c_row0, pv_c, pv_c_col = k_mla, v_mla, k_diff, N_LAT, p, P_VD // LANE
        mla_segs = [(k_mla, 0, v_mla, 0, 0, SEQ), (k_mla_c, 0, v_mla_c, 0, c_row0, CTX)]
        diff_segs = [(k_diff, 0, p, P_VD // LANE, 0, SEQ), (k_diff_c, 0, pv_c, pv_c_col, c_row0, CTX)]
        diff_kw = dict(lam_init=lam_init, lam=lam[l], g_sub=g_sub_l)
        o_mla = _attention("mla", q_mla, 0, 0, SEQ, mla_segs)
        o_diff = _attention("diff", q_diff, 0, 0, SEQ, diff_segs, **diff_kw)
        o_pool = _pool(p, 0, SEQ, w_pool_b[l], ps_l)
        if not last:
            o_mla = jnp.concatenate([o_mla, _attention("mla", q_mla, 0, N_LAT, CTX, mla_segs[1:])], axis=0)
            o_diff = jnp.concatenate([o_diff, _attention("diff", q_diff, 0, N_LAT, CTX, diff_segs[1:], **diff_kw)], axis=0)
            o_pool = jnp.concatenate([o_pool, _pool(p, N_LAT, CTX, w_pool_b[l], ps_l)], axis=0)
        x_all = _merge(x_all, p, o_mla, o_diff, o_pool, mods, b_gate[l], w_br_mla_b[l], w_br_diff_b[l],
                       w_br_pool_b[l], w_o_b[l], ln1_g[l].reshape(1, D), ln1_b[l].reshape(1, D), n_out)

        v, route = _route(x_all, mods, w_route[l], b_route[l], n_out)
        block_exp, slot_tok, n_used, dest = _dispatch_indices(route, n_out)
        yb = _expert_ffn(v, block_exp, slot_tok, n_used, w_gu[l], w_dn[l])
        x_all = _combine(x_all, yb, dest, route, mods, ln2_g[l].reshape(1, D), ln2_b[l].reshape(1, D), n_out)
    return x_all.reshape(BATCH, SEQ, D)
```

```python
import functools
import math

import jax
import jax.numpy as jnp
import numpy as np
from jax import lax
from jax.experimental import pallas as pl
from jax.experimental.pallas import tpu as pltpu

F32 = jnp.float32
BF16 = jnp.bfloat16

D = 2048
BATCH = 4
SEQ = 2048
CTX = 256
DEPTH = 2
GRID_W = 64
N_LAT = BATCH * SEQ
N_CTX = BATCH * CTX
N_ALL = N_LAT + N_CTX
MLA_HEADS = 8
MLA_Q_RANK = 768
MLA_KV_RANK = 512
MLA_NOPE = 128
MLA_ROPE = 64
MLA_V = 128
MLA_HEAD_PAD = 256
DIFF_HEADS = 4
DIFF_HD = 64
DIFF_V = 128
POOL_WINDOWS = (2, 4, 8, 16)
POOL_GROUP = 128
POOL_W = 512
N_GROUPS = 8
EXPERTS_PER_GROUP = 8
N_EXPERTS = 64
TOP_K = 2
D_EXPERT = 512
ROPE_BASE = 10000.0
EPS = 1e-6
DN_ALPHA = (2 * DEPTH) ** 0.25
MOD_ROWS = 8
CTX_MOD_ROW = BATCH

P_GATE = 0
P_CQ = 6144
P_KROT = 6912
P_CKV = 7168
P_KD = 7680
P_VD = 8192
P_QD = 8704
P_POOL = 9216
P_W = 9728
P_KV0 = 6656
P_KV_W = P_QD - P_KV0

LANE = 128
VMEM_LIMIT = 56 * 1024 * 1024

TM_PROJ = 1024
TN_PROJ = 512
TM_PREP = 512
TQ = 256
TM_MERGE = 256
TM_ROUTE = 512
TM_COMB = 256
EXPERT_ROWS = 256
ROUTE_LANES = 128


def _cparams(sem, vmem=VMEM_LIMIT):
    return pltpu.CompilerParams(dimension_semantics=sem, vmem_limit_bytes=vmem)


def _mod_index(row_block, rows_per_block):
    return jnp.minimum((row_block * rows_per_block) // SEQ, CTX_MOD_ROW)


def _layer_norm_rows(x):
    mu = jnp.mean(x, axis=-1, keepdims=True)
    xc = x - mu
    var = jnp.mean(xc * xc, axis=-1, keepdims=True)
    return xc * lax.rsqrt(var + EPS)


def _rms_rows(x):
    return x * lax.rsqrt(jnp.mean(x * x, axis=-1, keepdims=True) + EPS)


def _adaln_kernel(c_ref, w_ref, b_ref, o_ref):
    c = c_ref[...]
    s = (c * jax.nn.sigmoid(c)).astype(BF16)
    o_ref[...] = jnp.dot(s, w_ref[...].astype(BF16), preferred_element_type=F32) + b_ref[...]


def _adaln(cc, w_ada, b_ada):
    tn = 1024
    n = w_ada.shape[-1]
    return pl.pallas_call(
        _adaln_kernel,
        grid=(DEPTH, n // tn),
        in_specs=[pl.BlockSpec((MOD_ROWS, D), lambda l, j: (0, 0)),
                  pl.BlockSpec((None, D, tn), lambda l, j: (l, 0, j)),
                  pl.BlockSpec((None, 1, tn), lambda l, j: (l, 0, j))],
        out_specs=pl.BlockSpec((None, MOD_ROWS, tn), lambda l, j: (l, 0, j)),
        out_shape=jax.ShapeDtypeStruct((DEPTH, MOD_ROWS, n), F32),
        compiler_params=_cparams(("arbitrary", "arbitrary")),
        name="adaln",
    )(cc, w_ada, b_ada.reshape(DEPTH, 1, n))


def _proj_kernel(x_ref, sh_ref, sc_ref, w_ref, o_ref, u_ref):
    @pl.when(pl.program_id(1) == 0)
    def _():
        u = _layer_norm_rows(x_ref[...]) * (1.0 + sc_ref[0]) + sh_ref[0]
        u_ref[...] = u.astype(BF16)

    o_ref[...] = jnp.dot(u_ref[...], w_ref[...], preferred_element_type=F32).astype(o_ref.dtype)


def _proj(x_all, mods, w, l, row0, nrows, col0, ncols):
    tm = min(TM_PROJ, nrows)
    rb0, cb0 = row0 // tm, col0 // TN_PROJ
    return pl.pallas_call(
        _proj_kernel,
        grid=(nrows // tm, ncols // TN_PROJ),
        in_specs=[pl.BlockSpec((tm, D), lambda i, j: (rb0 + i, 0)),
                  pl.BlockSpec((1, 1, D), lambda i, j: (_mod_index(rb0 + i, tm), 0, 0)),
                  pl.BlockSpec((1, 1, D), lambda i, j: (_mod_index(rb0 + i, tm), 0, 1)),
                  pl.BlockSpec((None, D, TN_PROJ), lambda i, j: (l, 0, cb0 + j))],
        out_specs=pl.BlockSpec((tm, TN_PROJ), lambda i, j: (i, j)),
        out_shape=jax.ShapeDtypeStruct((nrows, ncols), BF16),
        scratch_shapes=[pltpu.VMEM((tm, D), BF16)],
        compiler_params=_cparams(("arbitrary", "arbitrary")),
        name="proj_in",
    )(x_all, mods, mods, w)


def _rope128(x, cos, sin):
    lane = lax.broadcasted_iota(jnp.int32, x.shape, 1)
    nxt = pltpu.roll(x, LANE - MLA_ROPE // 4, 1)
    prv = pltpu.roll(x, MLA_ROPE // 4, 1)
    partner = jnp.where((lane % (MLA_ROPE // 2)) < MLA_ROPE // 4, nxt, prv)
    return x * cos + partner * sin


def _kvprep_kernel(ckv_ref, kd_ref, kr_ref, cos_ref, sin_ref, g_ref, w_ref, k_ref, v_ref, kdo_ref):
    cos, sin = cos_ref[...], sin_ref[...]
    c = (_rms_rows(ckv_ref[...].astype(F32)) * g_ref[...]).astype(BF16)
    kv = jnp.dot(c, w_ref[...], preferred_element_type=F32)
    krot = _rope128(kr_ref[...].astype(F32), cos, sin).astype(BF16)
    for h in range(MLA_HEADS):
        k_ref[:, h * MLA_HEAD_PAD:h * MLA_HEAD_PAD + MLA_NOPE] = kv[:, h * MLA_NOPE:(h + 1) * MLA_NOPE].astype(BF16)
        k_ref[:, h * MLA_HEAD_PAD + MLA_NOPE:(h + 1) * MLA_HEAD_PAD] = krot
    v_ref[...] = kv[:, MLA_HEADS * MLA_NOPE:].astype(BF16)
    for h in range(DIFF_HEADS):
        sl = slice(h * LANE, (h + 1) * LANE)
        kdo_ref[:, sl] = _rope128(kd_ref[:, sl].astype(F32), cos, sin).astype(BF16)


def _rope_block_index(i, tm, row0):
    r = row0 + i * tm
    return jnp.where(r < N_LAT, (r % SEQ) // tm, SEQ // tm)


def _kvprep(p, pcol0, row0, nrows, rope_cos, rope_sin, g_kv, w_ukv, l):
    tm = min(TM_PREP, nrows)
    rb0 = row0 // tm
    return pl.pallas_call(
        _kvprep_kernel,
        grid=(nrows // tm,),
        in_specs=[pl.BlockSpec((tm, MLA_KV_RANK), lambda i: (i, (P_CKV - pcol0) // MLA_KV_RANK)),
                  pl.BlockSpec((tm, 512), lambda i: (i, (P_KD - pcol0) // 512)),
                  pl.BlockSpec((tm, LANE), lambda i: (i, (P_KROT - pcol0) // LANE)),
                  pl.BlockSpec((tm, LANE), lambda i: (_rope_block_index(i, tm, row0), 0)),
                  pl.BlockSpec((tm, LANE), lambda i: (_rope_block_index(i, tm, row0), 0)),
                  pl.BlockSpec((1, MLA_KV_RANK), lambda i: (0, 0)),
                  pl.BlockSpec((None, MLA_KV_RANK, 2 * MLA_HEADS * MLA_NOPE), lambda i: (l, 0, 0))],
        out_specs=[pl.BlockSpec((tm, MLA_HEADS * MLA_HEAD_PAD), lambda i: (i, 0)),
                   pl.BlockSpec((tm, MLA_HEADS * MLA_V), lambda i: (i, 0)),
                   pl.BlockSpec((tm, 512), lambda i: (i, 0))],
        out_shape=[jax.ShapeDtypeStruct((nrows, MLA_HEADS * MLA_HEAD_PAD), BF16),
                   jax.ShapeDtypeStruct((nrows, MLA_HEADS * MLA_V), BF16),
                   jax.ShapeDtypeStruct((nrows, 512), BF16)],
        compiler_params=_cparams(("arbitrary",)),
        name="kv_prep",
    )(p, p, p, rope_cos, rope_sin, g_kv, w_ukv)


def _qprep_kernel(cq_ref, qd_ref, cos_ref, sin_ref, g_ref, w_ref, q_ref, qdo_ref):
    cos, sin = cos_ref[...], sin_ref[...]
    c = (_rms_rows(cq_ref[...].astype(F32)) * g_ref[...]).astype(BF16)
    q = jnp.dot(c, w_ref[...], preferred_element_type=F32)
    scale = (MLA_NOPE + MLA_ROPE) ** -0.5
    for h in range(MLA_HEADS):
        lo = h * MLA_HEAD_PAD
        q_ref[:, lo:lo + MLA_NOPE] = (q[:, lo:lo + MLA_NOPE] * scale).astype(BF16)
        q_ref[:, lo + MLA_NOPE:lo + MLA_HEAD_PAD] = (
            _rope128(q[:, lo + MLA_NOPE:lo + MLA_HEAD_PAD], cos, sin) * scale).astype(BF16)
    dscale = DIFF_HD ** -0.5
    for h in range(DIFF_HEADS):
        sl = slice(h * LANE, (h + 1) * LANE)
        qdo_ref[:, sl] = (_rope128(qd_ref[:, sl].astype(F32), cos, sin) * dscale).astype(BF16)


def _qprep(p, row0, nrows, rope_cos, rope_sin, g_q, w_uq, l):
    tm = min(TM_PREP, nrows)
    return pl.pallas_call(
        _qprep_kernel,
        grid=(nrows // tm,),
        in_specs=[pl.BlockSpec((tm, MLA_Q_RANK), lambda i: (i, P_CQ // MLA_Q_RANK)),
                  pl.BlockSpec((tm, 512), lambda i: (i, P_QD // 512)),
                  pl.BlockSpec((tm, LANE), lambda i: (_rope_block_index(i, tm, row0), 0)),
                  pl.BlockSpec((tm, LANE), lambda i: (_rope_block_index(i, tm, row0), 0)),
                  pl.BlockSpec((1, MLA_Q_RANK), lambda i: (0, 0)),
                  pl.BlockSpec((None, MLA_Q_RANK, MLA_HEADS * MLA_HEAD_PAD), lambda i: (l, 0, 0))],
        out_specs=[pl.BlockSpec((tm, MLA_HEADS * MLA_HEAD_PAD), lambda i: (i, 0)),
                   pl.BlockSpec((tm, 512), lambda i: (i, 0))],
        out_shape=[jax.ShapeDtypeStruct((nrows, MLA_HEADS * MLA_HEAD_PAD), BF16),
                   jax.ShapeDtypeStruct((nrows, 512), BF16)],
        compiler_params=_cparams(("arbitrary",)),
        name="q_prep",
    )(p, p, rope_cos, rope_sin, g_q, w_uq)


_NT = (((1,), (1,)), ((), ()))


def _softmax_parts(scores):
    m = scores[0].max(axis=-1, keepdims=True)
    for s in scores[1:]:
        m = jnp.maximum(m, s.max(axis=-1, keepdims=True))
    ps = [jnp.exp(s - m) for s in scores]
    tot = ps[0].sum(axis=-1, keepdims=True)
    for p in ps[1:]:
        tot = tot + p.sum(axis=-1, keepdims=True)
    return ps, 1.0 / tot


def _mla_attn_kernel(n_seg, q_ref, *refs):
    k_refs, v_refs, o_ref = refs[:n_seg], refs[n_seg:2 * n_seg], refs[2 * n_seg]
    q = q_ref[...]
    scores = [lax.dot_general(q, k[...], _NT, preferred_element_type=F32) for k in k_refs]
    ps, inv = _softmax_parts(scores)
    acc = jnp.dot(ps[0].astype(BF16), v_refs[0][...], preferred_element_type=F32)
    for p, v in zip(ps[1:], v_refs[1:]):
        acc = acc + jnp.dot(p.astype(BF16), v[...], preferred_element_type=F32)
    o_ref[...] = (acc * inv).astype(o_ref.dtype)


def _diff_attn_kernel(n_seg, lam_init, q_ref, lam_ref, g_ref, *refs):
    k_refs, v_refs, o_ref = refs[:n_seg], refs[n_seg:2 * n_seg], refs[2 * n_seg]
    q = q_ref[...]
    lane = lax.broadcasted_iota(jnp.int32, q.shape, 1)
    q1 = jnp.where(lane < DIFF_HD, q, jnp.zeros_like(q))
    q2 = jnp.where(lane < DIFF_HD, jnp.zeros_like(q), q)
    lam = lam_ref[...]
    lam_val = (jnp.exp(jnp.sum(lam[0:1] * lam[1:2], axis=-1, keepdims=True))
               - jnp.exp(jnp.sum(lam[2:3] * lam[3:4], axis=-1, keepdims=True))) + lam_init
    s1 = [lax.dot_general(q1, k[...], _NT, preferred_element_type=F32) for k in k_refs]
    s2 = [lax.dot_general(q2, k[...], _NT, preferred_element_type=F32) for k in k_refs]
    p1, inv1 = _softmax_parts(s1)
    p2, inv2 = _softmax_parts(s2)
    c2 = lam_val * inv2
    acc = None
    for a, b, v in zip(p1, p2, v_refs):
        w = (a * inv1 - b * c2).astype(BF16)
        t = jnp.dot(w, v[...], preferred_element_type=F32)
        acc = t if acc is None else acc + t
    o_ref[...] = (_rms_rows(acc) * g_ref[...] * (1.0 - lam_init)).astype(o_ref.dtype)


def _attention(kind, q, q_col0, q_row0, n_q_per_batch, kv_segs, lam_init=None, lam=None, g_sub=None):
    heads, dq = (MLA_HEADS, MLA_HEAD_PAD) if kind == "mla" else (DIFF_HEADS, LANE)
    tq = min(TQ, n_q_per_batch)
    nq = n_q_per_batch // tq
    qb0 = q_row0 // tq
    qcb = q_col0 // dq
    n_seg = len(kv_segs)
    in_specs = [pl.BlockSpec((tq, dq), lambda b, h, i: (qb0 + b * nq + i, qcb + h))]
    args = [q]
    if kind == "diff":
        in_specs += [pl.BlockSpec((4, DIFF_HD), lambda b, h, i: (0, 0)),
                     pl.BlockSpec((1, DIFF_V), lambda b, h, i: (0, 0))]
        args += [lam, g_sub]
    for (k_arr, kcb, _, _, row0, rpb) in kv_segs:
        in_specs.append(pl.BlockSpec((rpb, dq), functools.partial(
            lambda b, h, i, r0, c0: (r0 + b, c0 + h), r0=row0 // rpb, c0=kcb)))
        args.append(k_arr)
    for (_, _, v_arr, vcb, row0, rpb) in kv_segs:
        in_specs.append(pl.BlockSpec((rpb, LANE), functools.partial(
            lambda b, h, i, r0, c0: (r0 + b, c0 + h), r0=row0 // rpb, c0=vcb)))
        args.append(v_arr)
    if kind == "mla":
        body = functools.partial(_mla_attn_kernel, n_seg)
    else:
        body = functools.partial(_diff_attn_kernel, n_seg, lam_init)
    return pl.pallas_call(
        body,
        grid=(BATCH, heads, nq),
        in_specs=in_specs,
        out_specs=pl.BlockSpec((tq, LANE), lambda b, h, i: (b * nq + i, h)),
        out_shape=jax.ShapeDtypeStruct((BATCH * n_q_per_batch, heads * LANE), BF16),
        compiler_params=_cparams(("arbitrary", "arbitrary", "arbitrary")),
        name=kind + "_attn",
    )(*args)


def _pool_kernel(seq, x_ref, w_ref, s_ref, o_ref):
    row = lax.broadcasted_iota(jnp.int32, (seq, POOL_GROUP), 0)

    def down(x, k):
        return jnp.where(row >= k, pltpu.roll(x, k, 0), 0.0)

    def up(x, k):
        return jnp.where(row < seq - k, pltpu.roll(x, seq - k, 0), 0.0)

    t = row.astype(F32)
    for gi, w in enumerate(POOL_WINDOWS):
        half = w // 2
        sl = slice(gi * POOL_GROUP, (gi + 1) * POOL_GROUP)
        x = x_ref[:, sl].astype(F32)
        before, after, span = down(x, 1), x, 1
        while span < half:
            before = before + down(before, span)
            after = after + up(after, span)
            span *= 2
        cnt = jnp.minimum(t + half, float(seq)) - jnp.maximum(t - half, 0.0)
        pooled = ((before + after) / cnt - x).astype(BF16)
        y = jnp.dot(pooled, w_ref[gi], preferred_element_type=F32)
        o_ref[:, sl] = (y * s_ref[:, sl]).astype(o_ref.dtype)


def _pool(p, row0, seq, w_pool, pool_scale, l):
    rb0 = row0 // seq
    return pl.pallas_call(
        functools.partial(_pool_kernel, seq),
        grid=(BATCH,),
        in_specs=[pl.BlockSpec((seq, POOL_W), lambda b: (rb0 + b, P_POOL // POOL_W)),
                  pl.BlockSpec((None, len(POOL_WINDOWS), POOL_GROUP, POOL_GROUP), lambda b: (l, 0, 0, 0)),
                  pl.BlockSpec((1, POOL_W), lambda b: (0, 0))],
        out_specs=pl.BlockSpec((seq, POOL_W), lambda b: (b, 0)),
        out_shape=jax.ShapeDtypeStruct((BATCH * seq, POOL_W), BF16),
        compiler_params=_cparams(("arbitrary",)),
        name="pool_mix",
    )(p, w_pool, pool_scale)


def _deepnorm(x, y, gate, g, b):
    return _layer_norm_rows(DN_ALPHA * x + gate * y) * g + b


def _merge_kernel(x_ref, gt_ref, om_ref, od_ref, op_ref, bg_ref, wm_ref, wd_ref, wp_ref, wo_ref,
                  g1_ref, lg_ref, lb_ref, o_ref):
    merged = None
    for br, (o_br, w_br) in enumerate(((om_ref, wm_ref), (od_ref, wd_ref), (op_ref, wp_ref))):
        y = jnp.dot(o_br[...], w_br[...], preferred_element_type=F32)
        gate = jax.nn.sigmoid(gt_ref[:, br * D:(br + 1) * D].astype(F32) + bg_ref[br:br + 1, :])
        merged = gate * y if merged is None else merged + gate * y
    out = jnp.dot(merged.astype(BF16), wo_ref[...], preferred_element_type=F32)
    o_ref[...] = _deepnorm(x_ref[...], out, g1_ref[0], lg_ref[...], lb_ref[...])


def _merge(x_all, p, o_mla, o_diff, o_pool, mods, b_gate, w_br_mla, w_br_diff, w_br_pool, w_o, ln_g, ln_b, l, nrows):
    tm = TM_MERGE
    const = lambda i: (0, 0)
    layer = lambda i: (l, 0, 0)
    once = dict(pipeline_mode=pl.Buffered(1))
    return pl.pallas_call(
        _merge_kernel,
        grid=(nrows // tm,),
        in_specs=[pl.BlockSpec((tm, D), lambda i: (i, 0)),
                  pl.BlockSpec((tm, 3 * D), lambda i: (i, P_GATE // (3 * D))),
                  pl.BlockSpec((tm, MLA_HEADS * MLA_V), lambda i: (i, 0)),
                  pl.BlockSpec((tm, DIFF_HEADS * DIFF_V), lambda i: (i, 0)),
                  pl.BlockSpec((tm, POOL_W), lambda i: (i, 0)),
                  pl.BlockSpec((3, D), const, **once),
                  pl.BlockSpec((None, MLA_HEADS * MLA_V, D), layer, **once),
                  pl.BlockSpec((None, DIFF_HEADS * DIFF_V, D), layer, **once),
                  pl.BlockSpec((None, POOL_W, D), layer, **once),
                  pl.BlockSpec((None, D, D), layer, **once),
                  pl.BlockSpec((1, 1, D), lambda i: (_mod_index(i, tm), 0, 2)),
                  pl.BlockSpec((1, D), const, **once),
                  pl.BlockSpec((1, D), const, **once)],
        out_specs=pl.BlockSpec((tm, D), lambda i: (i, 0)),
        out_shape=jax.ShapeDtypeStruct((nrows, D), F32),
        compiler_params=_cparams(("arbitrary",)),
        name="merge_out",
    )(x_all, p, o_mla, o_diff, o_pool, b_gate, w_br_mla, w_br_diff, w_br_pool, w_o, mods, ln_g, ln_b)


def _route_kernel(x_ref, sh_ref, sc_ref, w_ref, b_ref, v_ref, r_ref):
    v = _layer_norm_rows(x_ref[...]) * (1.0 + sc_ref[0]) + sh_ref[0]
    v_ref[...] = v
    v_hi = v.astype(BF16)
    v_lo = (v - v_hi.astype(F32)).astype(BF16)
    t = jnp.dot(v_hi, w_ref[...], preferred_element_type=F32)
    logit = (t[:, :ROUTE_LANES] + t[:, ROUTE_LANES:]
             + jnp.dot(v_lo, w_ref[:, :ROUTE_LANES], preferred_element_type=F32) + b_ref[...])
    lane = lax.broadcasted_iota(jnp.int32, logit.shape, 1).astype(F32)
    neg = jnp.float32(-jnp.inf)
    big = jnp.float32(ROUTE_LANES)
    gl = jnp.where(lane < N_GROUPS, logit, neg)
    gmax = gl.max(axis=-1, keepdims=True)
    grp_p = 1.0 / jnp.exp(gl - gmax).sum(axis=-1, keepdims=True)
    grp = jnp.where(gl == gmax, lane, big).min(axis=-1, keepdims=True)
    grp_lo = N_GROUPS + EXPERTS_PER_GROUP * grp
    in_grp = (lane >= grp_lo) & (lane < grp_lo + EXPERTS_PER_GROUP)
    el = jnp.where(in_grp, logit, neg)
    t1 = el.max(axis=-1, keepdims=True)
    i1 = jnp.where(el == t1, lane, big).min(axis=-1, keepdims=True)
    el2 = jnp.where(lane == i1, neg, el)
    t2 = el2.max(axis=-1, keepdims=True)
    i2 = jnp.where(el2 == t2, lane, big).min(axis=-1, keepdims=True)
    e = jnp.exp(t2 - t1)
    w1 = grp_p / (1.0 + e)
    w2 = w1 * e
    r_ref[...] = jnp.where(lane == 0, i1 - N_GROUPS,
                           jnp.where(lane == 1, i2 - N_GROUPS,
                                     jnp.where(lane == 2, w1, jnp.where(lane == 3, w2, 0.0))))


def _route(x_all, mods, w_route, b_route, nrows):
    tm = TM_ROUTE
    return pl.pallas_call(
        _route_kernel,
        grid=(nrows // tm,),
        in_specs=[pl.BlockSpec((tm, D), lambda i: (i, 0)),
                  pl.BlockSpec((1, 1, D), lambda i: (_mod_index(i, tm), 0, 3)),
                  pl.BlockSpec((1, 1, D), lambda i: (_mod_index(i, tm), 0, 4)),
                  pl.BlockSpec((D, 2 * ROUTE_LANES), lambda i: (0, 0)),
                  pl.BlockSpec((1, ROUTE_LANES), lambda i: (0, 0))],
        out_specs=[pl.BlockSpec((tm, D), lambda i: (i, 0)),
                   pl.BlockSpec((tm, ROUTE_LANES), lambda i: (i, 0))],
        out_shape=[jax.ShapeDtypeStruct((nrows, D), F32),
                   jax.ShapeDtypeStruct((nrows, ROUTE_LANES), F32)],
        compiler_params=_cparams(("arbitrary",)),
        name="moe_route",
    )(x_all, mods, mods, w_route, b_route)


def _gather_rows(src_hbm, idx_ref, base, n, dst, sem):
    def body(r, carry):
        tok = idx_ref[base + r]
        pltpu.make_async_copy(src_hbm.at[pl.ds(tok, 1), :], dst.at[pl.ds(r, 1), :], sem).start()
        return carry
    lax.fori_loop(0, n, body, 0)


def _wait_rows(src_hbm, n, dst, sem):
    pltpu.make_async_copy(src_hbm.at[pl.ds(0, n), :], dst, sem).wait()


def _ffn_kernel(bexp_ref, slot_ref, nused_ref, v_hbm, wgu_ref, wdn_ref, o_ref, xbuf, wgu_bf, wdn_bf, sems):
    i = pl.program_id(0)
    n_used = nused_ref[0]
    cur = i % 2

    @pl.when(i == 0)
    def _():
        _gather_rows(v_hbm, slot_ref, 0, EXPERT_ROWS, xbuf.at[0], sems.at[0])

    @pl.when(i + 1 < n_used)
    def _():
        _gather_rows(v_hbm, slot_ref, (i + 1) * EXPERT_ROWS, EXPERT_ROWS, xbuf.at[1 - cur], sems.at[1 - cur])

    @pl.when((i == 0) | (bexp_ref[i] != bexp_ref[jnp.maximum(i - 1, 0)]))
    def _():
        wgu_bf[...] = wgu_ref[...].astype(BF16)
        wdn_bf[...] = wdn_ref[...].astype(BF16)

    @pl.when((i < n_used) | (i == 0))
    def _():
        _wait_rows(v_hbm, EXPERT_ROWS, xbuf.at[cur], sems.at[cur])

    @pl.when(i < n_used)
    def _():
        x = xbuf[cur].astype(BF16)
        h = jnp.dot(x, wgu_bf[...], preferred_element_type=F32)
        gate, up = h[:, :D_EXPERT], h[:, D_EXPERT:]
        a = (gate * jax.nn.sigmoid(gate) * up).astype(BF16)
        o_ref[...] = jnp.dot(a, wdn_bf[...], preferred_element_type=F32)

    @pl.when(i >= n_used)
    def _():
        o_ref[...] = jnp.zeros_like(o_ref)


def _expert_ffn(v, block_exp, slot_tok, n_used, w_gu, w_dn, l):
    n_blocks = block_exp.shape[0]
    grid_spec = pltpu.PrefetchScalarGridSpec(
        num_scalar_prefetch=3,
        grid=(n_blocks,),
        in_specs=[pl.BlockSpec(memory_space=pl.ANY),
                  pl.BlockSpec((None, None, D, 2 * D_EXPERT), lambda i, be, st, nu: (l, be[i], 0, 0)),
                  pl.BlockSpec((None, None, D_EXPERT, D), lambda i, be, st, nu: (l, be[i], 0, 0))],
        out_specs=pl.BlockSpec((EXPERT_ROWS, D), lambda i, be, st, nu: (i, 0)),
        scratch_shapes=[pltpu.VMEM((2, EXPERT_ROWS, D), F32),
                        pltpu.VMEM((D, 2 * D_EXPERT), BF16),
                        pltpu.VMEM((D_EXPERT, D), BF16),
                        pltpu.SemaphoreType.DMA((2,))],
    )
    return pl.pallas_call(
        _ffn_kernel,
        grid_spec=grid_spec,
        out_shape=jax.ShapeDtypeStruct((n_blocks * EXPERT_ROWS, D), F32),
        compiler_params=_cparams(("arbitrary",)),
        name="expert_ffn",
    )(block_exp, slot_tok, n_used, v, w_gu, w_dn)


def _combine_kernel(dest_ref, y_hbm, x_ref, r_ref, g2_ref, lg_ref, lb_ref, o_ref, ybuf, sems):
    i = pl.program_id(0)
    n = pl.num_programs(0)
    tm = x_ref.shape[0]
    cur = i % 2

    @pl.when(i == 0)
    def _():
        _gather_rows(y_hbm, dest_ref, 0, TOP_K * tm, ybuf.at[0], sems.at[0])

    @pl.when(i + 1 < n)
    def _():
        _gather_rows(y_hbm, dest_ref, (i + 1) * TOP_K * tm, TOP_K * tm, ybuf.at[1 - cur], sems.at[1 - cur])

    _wait_rows(y_hbm, TOP_K * tm, ybuf.at[cur], sems.at[cur])
    r = r_ref[...]
    y = ybuf[cur, 0:tm, :] * r[:, 2:3] + ybuf[cur, tm:2 * tm, :] * r[:, 3:4]
    o_ref[...] = _deepnorm(x_ref[...], y, g2_ref[0], lg_ref[...], lb_ref[...])


def _combine(x_all, yb, dest, route, mods, ln_g, ln_b, nrows):
    tm = TM_COMB
    grid_spec = pltpu.PrefetchScalarGridSpec(
        num_scalar_prefetch=1,
        grid=(nrows // tm,),
        in_specs=[pl.BlockSpec(memory_space=pl.ANY),
                  pl.BlockSpec((tm, D), lambda i, d: (i, 0)),
                  pl.BlockSpec((tm, ROUTE_LANES), lambda i, d: (i, 0)),
                  pl.BlockSpec((1, 1, D), lambda i, d: (_mod_index(i, tm), 0, 5)),
                  pl.BlockSpec((1, D), lambda i, d: (0, 0)),
                  pl.BlockSpec((1, D), lambda i, d: (0, 0))],
        out_specs=pl.BlockSpec((tm, D), lambda i, d: (i, 0)),
        scratch_shapes=[pltpu.VMEM((2, TOP_K * tm, D), F32),
                        pltpu.SemaphoreType.DMA((2,))],
    )
    return pl.pallas_call(
        _combine_kernel,
        grid_spec=grid_spec,
        out_shape=jax.ShapeDtypeStruct((nrows, D), F32),
        compiler_params=_cparams(("arbitrary",)),
        name="moe_combine",
    )(dest, yb, x_all, route, mods, ln_g, ln_b)


def _dispatch_indices(route, n_tok):
    n_assign = n_tok * TOP_K
    flat_e = route[:, :TOP_K].astype(jnp.int32).reshape(n_assign)
    order = jnp.argsort(flat_e).astype(jnp.int32)
    rank = jnp.argsort(order).astype(jnp.int32)
    sorted_e = flat_e[order]
    experts = jnp.arange(N_EXPERTS, dtype=jnp.int32)
    start = jnp.searchsorted(sorted_e, experts, side="left").astype(jnp.int32)
    counts = jnp.searchsorted(sorted_e, experts, side="right").astype(jnp.int32) - start
    padded = (counts + EXPERT_ROWS - 1) // EXPERT_ROWS * EXPERT_ROWS
    pad_end = jnp.cumsum(padded)
    pad_start = pad_end - padded
    dest = rank + (pad_start - start)[flat_e]
    n_blocks = -(-(n_assign + N_EXPERTS * (EXPERT_ROWS - 1)) // EXPERT_ROWS)
    n_used = (pad_end[-1] // EXPERT_ROWS).astype(jnp.int32)
    blk = jnp.arange(n_blocks, dtype=jnp.int32)
    block_exp = jnp.minimum(jnp.searchsorted(pad_end, blk * EXPERT_ROWS, side="right"), N_EXPERTS - 1).astype(jnp.int32)
    block_exp = jnp.where(blk < n_used, block_exp, block_exp[jnp.maximum(n_used - 1, 0)])
    slot = jnp.arange(n_blocks * EXPERT_ROWS, dtype=jnp.int32)
    slot_e = jnp.repeat(block_exp, EXPERT_ROWS)
    j = slot - pad_start[slot_e]
    valid = (j < counts[slot_e]) & (slot < n_used * EXPERT_ROWS)
    src = jnp.clip(start[slot_e] + j, 0, n_assign - 1)
    slot_tok = jnp.where(valid, order[src] // TOP_K, 0).astype(jnp.int32)
    dest_blocks = dest.astype(jnp.int32).reshape(n_tok // TM_COMB, TM_COMB, TOP_K).transpose(0, 2, 1).reshape(n_assign)
    return block_exp, slot_tok, n_used.reshape(1), dest_blocks


_IN_OFFS = dict(zip(("c_kv", "k_rot", "k_diff", "v_diff", "c_q", "q_diff", "pool", "gate"),
                    np.cumsum((0, MLA_KV_RANK, MLA_ROPE, 512, 512, MLA_Q_RANK, 512, POOL_W)).tolist()))
_IN_MOVES = ((_IN_OFFS["gate"], 3 * D, P_GATE), (_IN_OFFS["c_q"], MLA_Q_RANK, P_CQ),
             (_IN_OFFS["c_kv"], MLA_KV_RANK, P_CKV), (_IN_OFFS["k_diff"], 512, P_KD),
             (_IN_OFFS["v_diff"], 512, P_VD), (_IN_OFFS["q_diff"], 512, P_QD), (_IN_OFFS["pool"], POOL_W, P_POOL))
IN_COLS = _IN_OFFS["gate"] + 3 * D


def _relayout_in_kernel(w_ref, o_ref):
    for src, width, dst in _IN_MOVES:
        o_ref[:, dst:dst + width] = w_ref[:, src:src + width].astype(BF16)
    kr = w_ref[:, _IN_OFFS["k_rot"]:_IN_OFFS["k_rot"] + LANE]
    lane = lax.broadcasted_iota(jnp.int32, kr.shape, 1)
    o_ref[:, P_KROT:P_KROT + LANE] = jnp.where(lane < MLA_ROPE, kr, 0.0).astype(BF16)
    o_ref[:, P_KROT + LANE:P_CKV] = jnp.zeros((kr.shape[0], P_CKV - P_KROT - LANE), BF16)


def _relayout_w_in(w_in):
    tr = 256
    return pl.pallas_call(
        _relayout_in_kernel,
        grid=(DEPTH, D // tr),
        in_specs=[pl.BlockSpec((None, tr, IN_COLS), lambda l, i: (l, i, 0))],
        out_specs=pl.BlockSpec((None, tr, P_W), lambda l, i: (l, i, 0)),
        out_shape=jax.ShapeDtypeStruct((DEPTH, D, P_W), BF16),
        compiler_params=_cparams(("arbitrary", "arbitrary")),
        name="relayout_w_in",
    )(w_in)


def _relayout_w_uq(w_uq):
    w = w_uq.reshape(DEPTH, MLA_Q_RANK, MLA_HEADS, MLA_NOPE + MLA_ROPE)
    w = jnp.pad(w, ((0, 0), (0, 0), (0, 0), (0, MLA_HEAD_PAD - MLA_NOPE - MLA_ROPE)))
    return w.reshape(DEPTH, MLA_Q_RANK, MLA_HEADS * MLA_HEAD_PAD).astype(BF16)


def _relayout_w_ukv(w_ukv):
    w = w_ukv.reshape(DEPTH, MLA_KV_RANK, MLA_HEADS, MLA_NOPE + MLA_V)
    k = w[..., :MLA_NOPE].reshape(DEPTH, MLA_KV_RANK, MLA_HEADS * MLA_NOPE)
    v = w[..., MLA_NOPE:].reshape(DEPTH, MLA_KV_RANK, MLA_HEADS * MLA_V)
    return jnp.concatenate([k, v], axis=-1).astype(BF16)


def _relayout_router(w_grp, b_grp, w_exp, b_exp):
    w = jnp.concatenate([w_grp, w_exp], axis=-1)
    w = jnp.pad(w, ((0, 0), (0, 0), (0, ROUTE_LANES - w.shape[-1])))
    w_hi = w.astype(BF16)
    w_lo = (w - w_hi.astype(F32)).astype(BF16)
    b = jnp.pad(jnp.concatenate([b_grp, b_exp], axis=-1), ((0, 0), (0, ROUTE_LANES - N_GROUPS - N_EXPERTS)))
    return jnp.concatenate([w_hi, w_lo], axis=-1), b.reshape(DEPTH, 1, ROUTE_LANES)


def _rope_tables():
    quarter = MLA_ROPE // 4
    inv_freq = ROPE_BASE ** (-jnp.arange(quarter, dtype=F32) / quarter)
    pos = jnp.arange(SEQ)
    row = (pos // GRID_W).astype(F32)[:, None] * inv_freq
    col = (pos % GRID_W).astype(F32)[:, None] * inv_freq
    cos64 = jnp.concatenate([jnp.cos(row), jnp.cos(row), jnp.cos(col), jnp.cos(col)], axis=-1)
    sin64 = jnp.concatenate([-jnp.sin(row), jnp.sin(row), -jnp.sin(col), jnp.sin(col)], axis=-1)
    cos = jnp.concatenate([jnp.tile(cos64, (1, 2)), jnp.ones((TM_PREP, LANE), F32)], axis=0)
    sin = jnp.concatenate([jnp.tile(sin64, (1, 2)), jnp.zeros((TM_PREP, LANE), F32)], axis=0)
    return cos, sin


def kernel(x, c, ctx, c_ctx, w_ada, b_ada, w_in, b_gate, g_q, g_kv, w_uq, w_ukv, lam, g_sub, w_pool, pool_scale,
           w_br_mla, w_br_diff, w_br_pool, w_o, ln1_g, ln1_b, w_grp, b_grp, w_exp, b_exp, w_gu, w_dn, ln2_g, ln2_b):
    assert x.shape == (BATCH, SEQ, D) and ctx.shape == (BATCH, CTX, D)
    assert MLA_ROPE == DIFF_HD
    cc = jnp.concatenate([c, c_ctx[None], jnp.zeros((MOD_ROWS - BATCH - 1, D), F32)], axis=0)
    ada = _adaln(cc, w_ada, b_ada)
    w_in_p = _relayout_w_in(w_in)
    w_uq_p = _relayout_w_uq(w_uq)
    w_ukv_p = _relayout_w_ukv(w_ukv)
    w_route, b_route = _relayout_router(w_grp, b_grp, w_exp, b_exp)
    rope_cos, rope_sin = _rope_tables()
    bf = lambda a: a.astype(BF16)
    w_pool_b, w_br_mla_b, w_br_diff_b, w_br_pool_b, w_o_b = map(bf, (w_pool, w_br_mla, w_br_diff, w_br_pool, w_o))

    x_all = jnp.concatenate([x.reshape(N_LAT, D), ctx.reshape(N_CTX, D)], axis=0)
    for l in range(DEPTH):
        last = l == DEPTH - 1
        lam_init = 0.8 - 0.6 * math.exp(-0.3 * l)
        mods = ada[l].reshape(MOD_ROWS, 1, 6 * D)
        n_out = N_LAT if last else N_ALL
        g_kv_l, g_q_l = g_kv[l].reshape(1, -1), g_q[l].reshape(1, -1)
        g_sub_l, ps_l = g_sub[l].reshape(1, -1), pool_scale[l].reshape(1, -1)

        p = _proj(x_all, mods, w_in_p, l, 0, n_out, 0, P_W)
        k_mla, v_mla, k_diff = _kvprep(p, 0, 0, n_out, rope_cos, rope_sin, g_kv_l, w_ukv_p, l)
        q_mla, q_diff = _qprep(p, 0, n_out, rope_cos, rope_sin, g_q_l, w_uq_p, l)
        if last:
            p_c = _proj(x_all, mods, w_in_p, l, N_LAT, N_CTX, P_KV0, P_KV_W)
            k_mla_c, v_mla_c, k_diff_c = _kvprep(p_c, P_KV0, N_LAT, N_CTX, rope_cos, rope_sin, g_kv_l, w_ukv_p, l)
            c_row0, pv_c, pv_c_col = 0, p_c, (P_VD - P_KV0) // LANE
        else:
            k_mla_c, v_mla_c, k_diff_c, c_row0, pv_c, pv_c_col = k_mla, v_mla, k_diff, N_LAT, p, P_VD // LANE
        mla_segs = [(k_mla, 0, v_mla, 0, 0, SEQ), (k_mla_c, 0, v_mla_c, 0, c_row0, CTX)]
        diff_segs = [(k_diff, 0, p, P_VD // LANE, 0, SEQ), (k_diff_c, 0, pv_c, pv_c_col, c_row0, CTX)]
        diff_kw = dict(lam_init=lam_init, lam=lam[l], g_sub=g_sub_l)
        o_mla = _attention("mla", q_mla, 0, 0, SEQ, mla_segs)
        o_diff = _attention("diff", q_diff, 0, 0, SEQ, diff_segs, **diff_kw)
        o_pool = _pool(p, 0, SEQ, w_pool_b, ps_l, l)
        if not last:
            o_mla = jnp.concatenate([o_mla, _attention("mla", q_mla, 0, N_LAT, CTX, mla_segs[1:])], axis=0)
            o_diff = jnp.concatenate([o_diff, _attention("diff", q_diff, 0, N_LAT, CTX, diff_segs[1:], **diff_kw)], axis=0)
            o_pool = jnp.concatenate([o_pool, _pool(p, N_LAT, CTX, w_pool_b, ps_l, l)], axis=0)
        x_all = _merge(x_all, p, o_mla, o_diff, o_pool, mods, b_gate[l], w_br_mla_b, w_br_diff_b,
                       w_br_pool_b, w_o_b, ln1_g[l].reshape(1, D), ln1_b[l].reshape(1, D), l, n_out)

        v, route = _route(x_all, mods, w_route[l], b_route[l], n_out)
        block_exp, slot_tok, n_used, dest = _dispatch_indices(route, n_out)
        yb = _expert_ffn(v, block_exp, slot_tok, n_used, w_gu, w_dn, l)
        x_all = _combine(x_all, yb, dest, route, mods, ln2_g[l].reshape(1, D), ln2_b[l].reshape(1, D), n_out)
    return x_all.reshape(BATCH, SEQ, D)
```

```python
import functools
import math

import jax
import jax.numpy as jnp
import numpy as np
from jax import lax
from jax.experimental import pallas as pl
from jax.experimental.pallas import tpu as pltpu

F32 = jnp.float32
BF16 = jnp.bfloat16

D = 2048
BATCH = 4
SEQ = 2048
CTX = 256
DEPTH = 2
GRID_W = 64
N_LAT = BATCH * SEQ
N_CTX = BATCH * CTX
N_ALL = N_LAT + N_CTX
MLA_HEADS = 8
MLA_Q_RANK = 768
MLA_KV_RANK = 512
MLA_NOPE = 128
MLA_ROPE = 64
MLA_V = 128
MLA_HEAD_PAD = 256
DIFF_HEADS = 4
DIFF_HD = 64
DIFF_V = 128
POOL_WINDOWS = (2, 4, 8, 16)
POOL_GROUP = 128
POOL_W = 512
N_GROUPS = 8
EXPERTS_PER_GROUP = 8
N_EXPERTS = 64
TOP_K = 2
D_EXPERT = 512
ROPE_BASE = 10000.0
EPS = 1e-6
DN_ALPHA = (2 * DEPTH) ** 0.25
MOD_ROWS = 8
CTX_MOD_ROW = BATCH

P_GATE = 0
P_CQ = 6144
P_KROT = 6912
P_CKV = 7168
P_KD = 7680
P_VD = 8192
P_QD = 8704
P_POOL = 9216
P_W = 9728
P_KV0 = 6656
P_KV_W = P_QD - P_KV0

LANE = 128
VMEM_LIMIT = 56 * 1024 * 1024

TM_PROJ = 1024
TN_PROJ = 512
TM_PREP = 512
TQ = 256
TM_MERGE = 256
TM_ROUTE = 512
TM_COMB = 256
EXPERT_ROWS = 256
ROUTE_LANES = 128


def _cparams(sem, vmem=VMEM_LIMIT):
    return pltpu.CompilerParams(dimension_semantics=sem, vmem_limit_bytes=vmem)


def _mod_index(row_block, rows_per_block):
    return jnp.minimum((row_block * rows_per_block) // SEQ, CTX_MOD_ROW)


def _layer_norm_rows(x):
    mu = jnp.mean(x, axis=-1, keepdims=True)
    xc = x - mu
    var = jnp.mean(xc * xc, axis=-1, keepdims=True)
    return xc * lax.rsqrt(var + EPS)


def _rms_rows(x):
    return x * lax.rsqrt(jnp.mean(x * x, axis=-1, keepdims=True) + EPS)


def _adaln_kernel(c_ref, w_ref, b_ref, o_ref):
    c = c_ref[...]
    s = (c * jax.nn.sigmoid(c)).astype(BF16)
    o_ref[...] = jnp.dot(s, w_ref[...].astype(BF16), preferred_element_type=F32) + b_ref[...]


def _adaln(cc, w_ada, b_ada):
    tn = 1024
    n = w_ada.shape[-1]
    return pl.pallas_call(
        _adaln_kernel,
        grid=(DEPTH, n // tn),
        in_specs=[pl.BlockSpec((MOD_ROWS, D), lambda l, j: (0, 0)),
                  pl.BlockSpec((None, D, tn), lambda l, j: (l, 0, j)),
                  pl.BlockSpec((None, 1, tn), lambda l, j: (l, 0, j))],
        out_specs=pl.BlockSpec((None, MOD_ROWS, tn), lambda l, j: (l, 0, j)),
        out_shape=jax.ShapeDtypeStruct((DEPTH, MOD_ROWS, n), F32),
        compiler_params=_cparams(("arbitrary", "arbitrary")),
        name="adaln",
    )(cc, w_ada, b_ada.reshape(DEPTH, 1, n))


def _proj_kernel(x_ref, sh_ref, sc_ref, w_ref, o_ref, u_ref):
    @pl.when(pl.program_id(1) == 0)
    def _():
        u = _layer_norm_rows(x_ref[...]) * (1.0 + sc_ref[0]) + sh_ref[0]
        u_ref[...] = u.astype(BF16)

    o_ref[...] = jnp.dot(u_ref[...], w_ref[...], preferred_element_type=F32).astype(o_ref.dtype)


def _proj(x_all, mods, w, l, row0, nrows, col0, ncols):
    tm = min(TM_PROJ, nrows)
    rb0, cb0 = row0 // tm, col0 // TN_PROJ
    return pl.pallas_call(
        _proj_kernel,
        grid=(nrows // tm, ncols // TN_PROJ),
        in_specs=[pl.BlockSpec((tm, D), lambda i, j: (rb0 + i, 0)),
                  pl.BlockSpec((1, 1, D), lambda i, j: (_mod_index(rb0 + i, tm), 0, 0)),
                  pl.BlockSpec((1, 1, D), lambda i, j: (_mod_index(rb0 + i, tm), 0, 1)),
                  pl.BlockSpec((None, D, TN_PROJ), lambda i, j: (l, 0, cb0 + j))],
        out_specs=pl.BlockSpec((tm, TN_PROJ), lambda i, j: (i, j)),
        out_shape=jax.ShapeDtypeStruct((nrows, ncols), BF16),
        scratch_shapes=[pltpu.VMEM((tm, D), BF16)],
        compiler_params=_cparams(("arbitrary", "arbitrary")),
        name="proj_in",
    )(x_all, mods, mods, w)


def _rope128(x, cos, sin):
    lane = lax.broadcasted_iota(jnp.int32, x.shape, 1)
    nxt = pltpu.roll(x, LANE - MLA_ROPE // 4, 1)
    prv = pltpu.roll(x, MLA_ROPE // 4, 1)
    partner = jnp.where((lane % (MLA_ROPE // 2)) < MLA_ROPE // 4, nxt, prv)
    return x * cos + partner * sin


def _kvprep_kernel(ckv_ref, kd_ref, kr_ref, cos_ref, sin_ref, g_ref, w_ref, k_ref, v_ref, kdo_ref):
    cos, sin = cos_ref[...], sin_ref[...]
    c = (_rms_rows(ckv_ref[...].astype(F32)) * g_ref[...]).astype(BF16)
    kv = jnp.dot(c, w_ref[...], preferred_element_type=F32)
    krot = _rope128(kr_ref[...].astype(F32), cos, sin).astype(BF16)
    for h in range(MLA_HEADS):
        k_ref[:, h * MLA_HEAD_PAD:h * MLA_HEAD_PAD + MLA_NOPE] = kv[:, h * MLA_NOPE:(h + 1) * MLA_NOPE].astype(BF16)
        k_ref[:, h * MLA_HEAD_PAD + MLA_NOPE:(h + 1) * MLA_HEAD_PAD] = krot
    v_ref[...] = kv[:, MLA_HEADS * MLA_NOPE:].astype(BF16)
    for h in range(DIFF_HEADS):
        sl = slice(h * LANE, (h + 1) * LANE)
        kdo_ref[:, sl] = _rope128(kd_ref[:, sl].astype(F32), cos, sin).astype(BF16)


def _rope_block_index(i, tm, row0):
    r = row0 + i * tm
    return jnp.where(r < N_LAT, (r % SEQ) // tm, SEQ // tm)


def _kvprep(p, pcol0, row0, nrows, rope_cos, rope_sin, g_kv, w_ukv, l):
    tm = min(TM_PREP, nrows)
    rb0 = row0 // tm
    return pl.pallas_call(
        _kvprep_kernel,
        grid=(nrows // tm,),
        in_specs=[pl.BlockSpec((tm, MLA_KV_RANK), lambda i: (i, (P_CKV - pcol0) // MLA_KV_RANK)),
                  pl.BlockSpec((tm, 512), lambda i: (i, (P_KD - pcol0) // 512)),
                  pl.BlockSpec((tm, LANE), lambda i: (i, (P_KROT - pcol0) // LANE)),
                  pl.BlockSpec((tm, LANE), lambda i: (_rope_block_index(i, tm, row0), 0)),
                  pl.BlockSpec((tm, LANE), lambda i: (_rope_block_index(i, tm, row0), 0)),
                  pl.BlockSpec((1, MLA_KV_RANK), lambda i: (0, 0)),
                  pl.BlockSpec((None, MLA_KV_RANK, 2 * MLA_HEADS * MLA_NOPE), lambda i: (l, 0, 0))],
        out_specs=[pl.BlockSpec((tm, MLA_HEADS * MLA_HEAD_PAD), lambda i: (i, 0)),
                   pl.BlockSpec((tm, MLA_HEADS * MLA_V), lambda i: (i, 0)),
                   pl.BlockSpec((tm, 512), lambda i: (i, 0))],
        out_shape=[jax.ShapeDtypeStruct((nrows, MLA_HEADS * MLA_HEAD_PAD), BF16),
                   jax.ShapeDtypeStruct((nrows, MLA_HEADS * MLA_V), BF16),
                   jax.ShapeDtypeStruct((nrows, 512), BF16)],
        compiler_params=_cparams(("arbitrary",)),
        name="kv_prep",
    )(p, p, p, rope_cos, rope_sin, g_kv, w_ukv)


def _qprep_kernel(cq_ref, qd_ref, cos_ref, sin_ref, g_ref, w_ref, q_ref, qdo_ref):
    cos, sin = cos_ref[...], sin_ref[...]
    c = (_rms_rows(cq_ref[...].astype(F32)) * g_ref[...]).astype(BF16)
    q = jnp.dot(c, w_ref[...], preferred_element_type=F32)
    scale = (MLA_NOPE + MLA_ROPE) ** -0.5
    for h in range(MLA_HEADS):
        lo = h * MLA_HEAD_PAD
        q_ref[:, lo:lo + MLA_NOPE] = (q[:, lo:lo + MLA_NOPE] * scale).astype(BF16)
        q_ref[:, lo + MLA_NOPE:lo + MLA_HEAD_PAD] = (
            _rope128(q[:, lo + MLA_NOPE:lo + MLA_HEAD_PAD], cos, sin) * scale).astype(BF16)
    dscale = DIFF_HD ** -0.5
    for h in range(DIFF_HEADS):
        sl = slice(h * LANE, (h + 1) * LANE)
        qdo_ref[:, sl] = (_rope128(qd_ref[:, sl].astype(F32), cos, sin) * dscale).astype(BF16)


def _qprep(p, row0, nrows, rope_cos, rope_sin, g_q, w_uq, l):
    tm = min(TM_PREP, nrows)
    return pl.pallas_call(
        _qprep_kernel,
        grid=(nrows // tm,),
        in_specs=[pl.BlockSpec((tm, MLA_Q_RANK), lambda i: (i, P_CQ // MLA_Q_RANK)),
                  pl.BlockSpec((tm, 512), lambda i: (i, P_QD // 512)),
                  pl.BlockSpec((tm, LANE), lambda i: (_rope_block_index(i, tm, row0), 0)),
                  pl.BlockSpec((tm, LANE), lambda i: (_rope_block_index(i, tm, row0), 0)),
                  pl.BlockSpec((1, MLA_Q_RANK), lambda i: (0, 0)),
                  pl.BlockSpec((None, MLA_Q_RANK, MLA_HEADS * MLA_HEAD_PAD), lambda i: (l, 0, 0))],
        out_specs=[pl.BlockSpec((tm, MLA_HEADS * MLA_HEAD_PAD), lambda i: (i, 0)),
                   pl.BlockSpec((tm, 512), lambda i: (i, 0))],
        out_shape=[jax.ShapeDtypeStruct((nrows, MLA_HEADS * MLA_HEAD_PAD), BF16),
                   jax.ShapeDtypeStruct((nrows, 512), BF16)],
        compiler_params=_cparams(("arbitrary",)),
        name="q_prep",
    )(p, p, rope_cos, rope_sin, g_q, w_uq)


_NT = (((1,), (1,)), ((), ()))


def _softmax_parts(scores):
    m = scores[0].max(axis=-1, keepdims=True)
    for s in scores[1:]:
        m = jnp.maximum(m, s.max(axis=-1, keepdims=True))
    ps = [jnp.exp(s - m) for s in scores]
    tot = ps[0].sum(axis=-1, keepdims=True)
    for p in ps[1:]:
        tot = tot + p.sum(axis=-1, keepdims=True)
    return ps, 1.0 / tot


def _mla_attn_kernel(n_seg, q_ref, *refs):
    k_refs, v_refs, o_ref = refs[:n_seg], refs[n_seg:2 * n_seg], refs[2 * n_seg]
    q = q_ref[...]
    scores = [lax.dot_general(q, k[...], _NT, preferred_element_type=F32) for k in k_refs]
    ps, inv = _softmax_parts(scores)
    acc = jnp.dot(ps[0].astype(BF16), v_refs[0][...], preferred_element_type=F32)
    for p, v in zip(ps[1:], v_refs[1:]):
        acc = acc + jnp.dot(p.astype(BF16), v[...], preferred_element_type=F32)
    o_ref[...] = (acc * inv).astype(o_ref.dtype)


def _diff_attn_kernel(n_seg, lam_init, q_ref, lam_ref, g_ref, *refs):
    k_refs, v_refs, o_ref = refs[:n_seg], refs[n_seg:2 * n_seg], refs[2 * n_seg]
    q = q_ref[...]
    lane = lax.broadcasted_iota(jnp.int32, q.shape, 1)
    q1 = jnp.where(lane < DIFF_HD, q, jnp.zeros_like(q))
    q2 = jnp.where(lane < DIFF_HD, jnp.zeros_like(q), q)
    lam = lam_ref[...]
    lam_val = (jnp.exp(jnp.sum(lam[0:1] * lam[1:2], axis=-1, keepdims=True))
               - jnp.exp(jnp.sum(lam[2:3] * lam[3:4], axis=-1, keepdims=True))) + lam_init
    s1 = [lax.dot_general(q1, k[...], _NT, preferred_element_type=F32) for k in k_refs]
    s2 = [lax.dot_general(q2, k[...], _NT, preferred_element_type=F32) for k in k_refs]
    p1, inv1 = _softmax_parts(s1)
    p2, inv2 = _softmax_parts(s2)
    c2 = lam_val * inv2
    acc = None
    for a, b, v in zip(p1, p2, v_refs):
        w = (a * inv1 - b * c2).astype(BF16)
        t = jnp.dot(w, v[...], preferred_element_type=F32)
        acc = t if acc is None else acc + t
    o_ref[...] = (_rms_rows(acc) * g_ref[...] * (1.0 - lam_init)).astype(o_ref.dtype)


def _attention(kind, q, q_col0, q_row0, n_q_per_batch, kv_segs, lam_init=None, lam=None, g_sub=None):
    heads, dq = (MLA_HEADS, MLA_HEAD_PAD) if kind == "mla" else (DIFF_HEADS, LANE)
    tq = min(TQ, n_q_per_batch)
    nq = n_q_per_batch // tq
    qb0 = q_row0 // tq
    qcb = q_col0 // dq
    n_seg = len(kv_segs)
    in_specs = [pl.BlockSpec((tq, dq), lambda b, h, i: (qb0 + b * nq + i, qcb + h))]
    args = [q]
    if kind == "diff":
        in_specs += [pl.BlockSpec((4, DIFF_HD), lambda b, h, i: (0, 0)),
                     pl.BlockSpec((1, DIFF_V), lambda b, h, i: (0, 0))]
        args += [lam, g_sub]
    for (k_arr, kcb, _, _, row0, rpb) in kv_segs:
        in_specs.append(pl.BlockSpec((rpb, dq), functools.partial(
            lambda b, h, i, r0, c0: (r0 + b, c0 + h), r0=row0 // rpb, c0=kcb)))
        args.append(k_arr)
    for (_, _, v_arr, vcb, row0, rpb) in kv_segs:
        in_specs.append(pl.BlockSpec((rpb, LANE), functools.partial(
            lambda b, h, i, r0, c0: (r0 + b, c0 + h), r0=row0 // rpb, c0=vcb)))
        args.append(v_arr)
    if kind == "mla":
        body = functools.partial(_mla_attn_kernel, n_seg)
    else:
        body = functools.partial(_diff_attn_kernel, n_seg, lam_init)
    return pl.pallas_call(
        body,
        grid=(BATCH, heads, nq),
        in_specs=in_specs,
        out_specs=pl.BlockSpec((tq, LANE), lambda b, h, i: (b * nq + i, h)),
        out_shape=jax.ShapeDtypeStruct((BATCH * n_q_per_batch, heads * LANE), BF16),
        compiler_params=_cparams(("arbitrary", "arbitrary", "arbitrary")),
        name=kind + "_attn",
    )(*args)


def _pool_kernel(seq, x_ref, w_ref, s_ref, o_ref):
    row = lax.broadcasted_iota(jnp.int32, (seq, POOL_GROUP), 0)

    def down(x, k):
        return jnp.where(row >= k, pltpu.roll(x, k, 0), 0.0)

    def up(x, k):
        return jnp.where(row < seq - k, pltpu.roll(x, seq - k, 0), 0.0)

    t = row.astype(F32)
    for gi, w in enumerate(POOL_WINDOWS):
        half = w // 2
        sl = slice(gi * POOL_GROUP, (gi + 1) * POOL_GROUP)
        x = x_ref[:, sl].astype(F32)
        before, after, span = down(x, 1), x, 1
        while span < half:
            before = before + down(before, span)
            after = after + up(after, span)
            span *= 2
        cnt = jnp.minimum(t + half, float(seq)) - jnp.maximum(t - half, 0.0)
        pooled = ((before + after) / cnt - x).astype(BF16)
        y = jnp.dot(pooled, w_ref[gi], preferred_element_type=F32)
        o_ref[:, sl] = (y * s_ref[:, sl]).astype(o_ref.dtype)


def _pool(p, row0, seq, w_pool, pool_scale, l):
    rb0 = row0 // seq
    return pl.pallas_call(
        functools.partial(_pool_kernel, seq),
        grid=(BATCH,),
        in_specs=[pl.BlockSpec((seq, POOL_W), lambda b: (rb0 + b, P_POOL // POOL_W)),
                  pl.BlockSpec((None, len(POOL_WINDOWS), POOL_GROUP, POOL_GROUP), lambda b: (l, 0, 0, 0)),
                  pl.BlockSpec((1, POOL_W), lambda b: (0, 0))],
        out_specs=pl.BlockSpec((seq, POOL_W), lambda b: (b, 0)),
        out_shape=jax.ShapeDtypeStruct((BATCH * seq, POOL_W), BF16),
        compiler_params=_cparams(("arbitrary",)),
        name="pool_mix",
    )(p, w_pool, pool_scale)


def _deepnorm(x, y, gate, g, b):
    return _layer_norm_rows(DN_ALPHA * x + gate * y) * g + b


def _merge_kernel(x_ref, gt_ref, om_ref, od_ref, op_ref, bg_ref, wm_ref, wd_ref, wp_ref, wo_ref,
                  g1_ref, lg_ref, lb_ref, o_ref):
    merged = None
    for br, (o_br, w_br) in enumerate(((om_ref, wm_ref), (od_ref, wd_ref), (op_ref, wp_ref))):
        y = jnp.dot(o_br[...], w_br[...], preferred_element_type=F32)
        gate = jax.nn.sigmoid(gt_ref[:, br * D:(br + 1) * D].astype(F32) + bg_ref[br:br + 1, :])
        merged = gate * y if merged is None else merged + gate * y
    out = jnp.dot(merged.astype(BF16), wo_ref[...], preferred_element_type=F32)
    o_ref[...] = _deepnorm(x_ref[...], out, g1_ref[0], lg_ref[...], lb_ref[...])


def _merge(x_all, p, o_mla, o_diff, o_pool, mods, b_gate, w_br_mla, w_br_diff, w_br_pool, w_o, ln_g, ln_b, l, nrows):
    tm = TM_MERGE
    const = lambda i: (0, 0)
    layer = lambda i: (l, 0, 0)
    once = dict(pipeline_mode=pl.Buffered(1))
    return pl.pallas_call(
        _merge_kernel,
        grid=(nrows // tm,),
        in_specs=[pl.BlockSpec((tm, D), lambda i: (i, 0)),
                  pl.BlockSpec((tm, 3 * D), lambda i: (i, P_GATE // (3 * D))),
                  pl.BlockSpec((tm, MLA_HEADS * MLA_V), lambda i: (i, 0)),
                  pl.BlockSpec((tm, DIFF_HEADS * DIFF_V), lambda i: (i, 0)),
                  pl.BlockSpec((tm, POOL_W), lambda i: (i, 0)),
                  pl.BlockSpec((3, D), const, **once),
                  pl.BlockSpec((None, MLA_HEADS * MLA_V, D), layer, **once),
                  pl.BlockSpec((None, DIFF_HEADS * DIFF_V, D), layer, **once),
                  pl.BlockSpec((None, POOL_W, D), layer, **once),
                  pl.BlockSpec((None, D, D), layer, **once),
                  pl.BlockSpec((1, 1, D), lambda i: (_mod_index(i, tm), 0, 2)),
                  pl.BlockSpec((1, D), const, **once),
                  pl.BlockSpec((1, D), const, **once)],
        out_specs=pl.BlockSpec((tm, D), lambda i: (i, 0)),
        out_shape=jax.ShapeDtypeStruct((nrows, D), F32),
        compiler_params=_cparams(("arbitrary",)),
        name="merge_out",
    )(x_all, p, o_mla, o_diff, o_pool, b_gate, w_br_mla, w_br_diff, w_br_pool, w_o, mods, ln_g, ln_b)


_HI_MASK = -65536


def _pack_bf16_pairs(lo, hi):
    lo_bits = lax.bitcast_convert_type(lo.astype(BF16).astype(F32), jnp.int32)
    hi_bits = lax.bitcast_convert_type(hi.astype(BF16).astype(F32), jnp.int32)
    return hi_bits | lax.shift_right_logical(lo_bits, jnp.full_like(lo_bits, 16))


def _unpack_bf16_pairs(w):
    lo = lax.bitcast_convert_type(jnp.left_shift(w, 16), F32)
    hi = lax.bitcast_convert_type(w & _HI_MASK, F32)
    return lo, hi


def _route_kernel(x_ref, sh_ref, sc_ref, w_ref, b_ref, vp_ref, r_ref, cnt_ref, carry_ref):
    @pl.when(pl.program_id(0) == 0)
    def _():
        carry_ref[...] = jnp.zeros_like(carry_ref)

    v = _layer_norm_rows(x_ref[...]) * (1.0 + sc_ref[0]) + sh_ref[0]
    vp_ref[...] = _pack_bf16_pairs(v[:, :D // 2], v[:, D // 2:])
    v_hi = v.astype(BF16)
    v_lo = (v - v_hi.astype(F32)).astype(BF16)
    t = jnp.dot(v_hi, w_ref[...], preferred_element_type=F32)
    logit = (t[:, :ROUTE_LANES] + t[:, ROUTE_LANES:]
             + jnp.dot(v_lo, w_ref[:, :ROUTE_LANES], preferred_element_type=F32) + b_ref[...])
    lane = lax.broadcasted_iota(jnp.int32, logit.shape, 1).astype(F32)
    neg = jnp.float32(-jnp.inf)
    big = jnp.float32(ROUTE_LANES)
    gl = jnp.where(lane < N_GROUPS, logit, neg)
    gmax = gl.max(axis=-1, keepdims=True)
    grp_p = 1.0 / jnp.exp(gl - gmax).sum(axis=-1, keepdims=True)
    grp = jnp.where(gl == gmax, lane, big).min(axis=-1, keepdims=True)
    grp_lo = N_GROUPS + EXPERTS_PER_GROUP * grp
    in_grp = (lane >= grp_lo) & (lane < grp_lo + EXPERTS_PER_GROUP)
    el = jnp.where(in_grp, logit, neg)
    t1 = el.max(axis=-1, keepdims=True)
    i1 = jnp.where(el == t1, lane, big).min(axis=-1, keepdims=True)
    el2 = jnp.where(lane == i1, neg, el)
    t2 = el2.max(axis=-1, keepdims=True)
    i2 = jnp.where(el2 == t2, lane, big).min(axis=-1, keepdims=True)
    e = jnp.exp(t2 - t1)
    w1 = grp_p / (1.0 + e)
    w2 = w1 * e
    oh1, oh2 = lane == i1, lane == i2
    oh = jnp.where(oh1 | oh2, 1.0, 0.0)
    tm = oh.shape[0]
    tri = (lax.broadcasted_iota(jnp.int32, (tm, tm), 1) < lax.broadcasted_iota(jnp.int32, (tm, tm), 0))
    before = jnp.dot(jnp.where(tri, 1.0, 0.0).astype(BF16), oh.astype(BF16),
                     preferred_element_type=F32) + carry_ref[...]
    rank1 = jnp.sum(jnp.where(oh1, before, 0.0), axis=-1, keepdims=True)
    rank2 = jnp.sum(jnp.where(oh2, before, 0.0), axis=-1, keepdims=True)
    carry_ref[...] += jnp.sum(oh, axis=0, keepdims=True)
    cnt_ref[...] = carry_ref[...]
    fields = (i1 - N_GROUPS, i2 - N_GROUPS, w1, w2, rank1, rank2)
    out = jnp.zeros_like(logit)
    for k, f in enumerate(fields):
        out = jnp.where(lane == k, f, out)
    r_ref[...] = out


def _route(x_all, mods, w_route, b_route, nrows):
    tm = TM_ROUTE
    return pl.pallas_call(
        _route_kernel,
        grid=(nrows // tm,),
        in_specs=[pl.BlockSpec((tm, D), lambda i: (i, 0)),
                  pl.BlockSpec((1, 1, D), lambda i: (_mod_index(i, tm), 0, 3)),
                  pl.BlockSpec((1, 1, D), lambda i: (_mod_index(i, tm), 0, 4)),
                  pl.BlockSpec((D, 2 * ROUTE_LANES), lambda i: (0, 0)),
                  pl.BlockSpec((1, ROUTE_LANES), lambda i: (0, 0))],
        out_specs=[pl.BlockSpec((tm, D // 2), lambda i: (i, 0)),
                   pl.BlockSpec((tm, ROUTE_LANES), lambda i: (i, 0)),
                   pl.BlockSpec((1, ROUTE_LANES), lambda i: (0, 0))],
        out_shape=[jax.ShapeDtypeStruct((nrows, D // 2), jnp.int32),
                   jax.ShapeDtypeStruct((nrows, ROUTE_LANES), F32),
                   jax.ShapeDtypeStruct((1, ROUTE_LANES), F32)],
        scratch_shapes=[pltpu.VMEM((1, ROUTE_LANES), F32)],
        compiler_params=_cparams(("arbitrary",)),
        name="moe_route",
    )(x_all, mods, mods, w_route, b_route)


def _gather_rows(src_hbm, idx_ref, base, n, dst, sem):
    def body(r, carry):
        tok = idx_ref[base + r]
        pltpu.make_async_copy(src_hbm.at[pl.ds(tok, 1), :], dst.at[pl.ds(r, 1), :], sem).start()
        return carry
    lax.fori_loop(0, n, body, 0, unroll=8)


def _wait_rows(src_hbm, n, dst, sem):
    pltpu.make_async_copy(src_hbm.at[pl.ds(0, n), :], dst, sem).wait()


def _dispatch_kernel(dest_ref, zrow_ref, nused_ref, vp_ref, xs_hbm, zbuf, sems):
    i = pl.program_id(0)
    tm = vp_ref.shape[0]
    row_sem, zero_sem = sems.at[0], sems.at[1]
    n_blocks = xs_hbm.shape[0] // EXPERT_ROWS

    @pl.when(i == 0)
    def _():
        zbuf[...] = jnp.zeros_like(zbuf)

        def clear(row):
            pltpu.make_async_copy(zbuf, xs_hbm.at[pl.ds(pl.multiple_of(row, EXPERT_ROWS), EXPERT_ROWS), :],
                                  zero_sem).start()

        def wait_clear():
            pltpu.make_async_copy(zbuf, xs_hbm.at[pl.ds(0, EXPERT_ROWS), :], zero_sem).wait()

        def fill(e, carry):
            @pl.when(zrow_ref[e] >= 0)
            def _():
                clear(zrow_ref[e])
            return carry
        lax.fori_loop(0, N_EXPERTS, fill, 0)

        def fill_tail(b, carry):
            clear(b * EXPERT_ROWS)
            return carry
        lax.fori_loop(nused_ref[0], n_blocks, fill_tail, 0)

        def drain(e, carry):
            @pl.when(zrow_ref[e] >= 0)
            def _():
                wait_clear()
            return carry
        lax.fori_loop(0, N_EXPERTS, drain, 0)

        def drain_tail(b, carry):
            wait_clear()
            return carry
        lax.fori_loop(nused_ref[0], n_blocks, drain_tail, 0)

    def scatter(r, carry):
        for k in range(TOP_K):
            slot = dest_ref[(i * TOP_K + k) * tm + r]
            pltpu.make_async_copy(vp_ref.at[pl.ds(r, 1), :], xs_hbm.at[pl.ds(slot, 1), :], row_sem).start()
        return carry
    lax.fori_loop(0, tm, scatter, 0, unroll=8)
    for _ in range(TOP_K):
        pltpu.make_async_copy(vp_ref, xs_hbm.at[pl.ds(0, tm), :], row_sem).wait()


def _dispatch(vp, dest, zrow, n_used, n_rows):
    tm = TM_COMB
    grid_spec = pltpu.PrefetchScalarGridSpec(
        num_scalar_prefetch=3,
        grid=(vp.shape[0] // tm,),
        in_specs=[pl.BlockSpec((tm, D // 2), lambda i, d, z, nu: (i, 0))],
        out_specs=pl.BlockSpec(memory_space=pl.ANY),
        scratch_shapes=[pltpu.VMEM((EXPERT_ROWS, D // 2), jnp.int32),
                        pltpu.SemaphoreType.DMA((2,))],
    )
    return pl.pallas_call(
        _dispatch_kernel,
        grid_spec=grid_spec,
        out_shape=jax.ShapeDtypeStruct((n_rows, D // 2), jnp.int32),
        compiler_params=_cparams(("arbitrary",)),
        name="moe_dispatch",
    )(dest, zrow, n_used, vp)


def _ffn_kernel(bexp_ref, nused_ref, xs_ref, wgu_ref, wdn_ref, o_ref, wgu_bf, wdn_bf):
    i = pl.program_id(0)
    n_used = nused_ref[0]

    @pl.when((i == 0) | (bexp_ref[i] != bexp_ref[jnp.maximum(i - 1, 0)]))
    def _():
        wgu_bf[...] = wgu_ref[...].astype(BF16)
        wdn_bf[...] = wdn_ref[...].astype(BF16)

    @pl.when(i < n_used)
    def _():
        lo, hi = _unpack_bf16_pairs(xs_ref[...])
        h = (jnp.dot(lo.astype(BF16), wgu_bf[:D // 2, :], preferred_element_type=F32)
             + jnp.dot(hi.astype(BF16), wgu_bf[D // 2:, :], preferred_element_type=F32))
        gate, up = h[:, :D_EXPERT], h[:, D_EXPERT:]
        a = (gate * jax.nn.sigmoid(gate) * up).astype(BF16)
        y = jnp.dot(a, wdn_bf[...], preferred_element_type=F32)
        o_ref[...] = _pack_bf16_pairs(y[:, :D // 2], y[:, D // 2:])

    @pl.when(i >= n_used)
    def _():
        o_ref[...] = jnp.zeros_like(o_ref)


def _expert_ffn(xs, block_exp, n_used, w_gu, w_dn, l):
    n_blocks = block_exp.shape[0]
    grid_spec = pltpu.PrefetchScalarGridSpec(
        num_scalar_prefetch=2,
        grid=(n_blocks,),
        in_specs=[pl.BlockSpec((EXPERT_ROWS, D // 2), lambda i, be, nu: (jnp.minimum(i, nu[0] - 1), 0)),
                  pl.BlockSpec((None, None, D, 2 * D_EXPERT), lambda i, be, nu: (l, be[i], 0, 0)),
                  pl.BlockSpec((None, None, D_EXPERT, D), lambda i, be, nu: (l, be[i], 0, 0))],
        out_specs=pl.BlockSpec((EXPERT_ROWS, D // 2), lambda i, be, nu: (i, 0)),
        scratch_shapes=[pltpu.VMEM((D, 2 * D_EXPERT), BF16),
                        pltpu.VMEM((D_EXPERT, D), BF16)],
    )
    return pl.pallas_call(
        _ffn_kernel,
        grid_spec=grid_spec,
        out_shape=jax.ShapeDtypeStruct((n_blocks * EXPERT_ROWS, D // 2), jnp.int32),
        compiler_params=_cparams(("arbitrary",)),
        name="expert_ffn",
    )(block_exp, n_used, xs, w_gu, w_dn)


def _combine_kernel(dest_ref, y_hbm, x_ref, r_ref, g2_ref, lg_ref, lb_ref, o_ref, ybuf, sems):
    i = pl.program_id(0)
    n = pl.num_programs(0)
    tm = x_ref.shape[0]
    cur = i % 2

    @pl.when(i == 0)
    def _():
        _gather_rows(y_hbm, dest_ref, 0, TOP_K * tm, ybuf.at[0], sems.at[0])

    @pl.when(i + 1 < n)
    def _():
        _gather_rows(y_hbm, dest_ref, (i + 1) * TOP_K * tm, TOP_K * tm, ybuf.at[1 - cur], sems.at[1 - cur])

    _wait_rows(y_hbm, TOP_K * tm, ybuf.at[cur], sems.at[cur])
    r = r_ref[...]
    w1, w2 = r[:, 2:3], r[:, 3:4]
    lo1, hi1 = _unpack_bf16_pairs(ybuf[cur, 0:tm, :])
    lo2, hi2 = _unpack_bf16_pairs(ybuf[cur, tm:2 * tm, :])
    y = jnp.concatenate([lo1 * w1 + lo2 * w2, hi1 * w1 + hi2 * w2], axis=-1)
    o_ref[...] = _deepnorm(x_ref[...], y, g2_ref[0], lg_ref[...], lb_ref[...])


def _combine(x_all, yb, dest, route, mods, ln_g, ln_b, nrows):
    tm = TM_COMB
    grid_spec = pltpu.PrefetchScalarGridSpec(
        num_scalar_prefetch=1,
        grid=(nrows // tm,),
        in_specs=[pl.BlockSpec(memory_space=pl.ANY),
                  pl.BlockSpec((tm, D), lambda i, d: (i, 0)),
                  pl.BlockSpec((tm, ROUTE_LANES), lambda i, d: (i, 0)),
                  pl.BlockSpec((1, 1, D), lambda i, d: (_mod_index(i, tm), 0, 5)),
                  pl.BlockSpec((1, D), lambda i, d: (0, 0)),
                  pl.BlockSpec((1, D), lambda i, d: (0, 0))],
        out_specs=pl.BlockSpec((tm, D), lambda i, d: (i, 0)),
        scratch_shapes=[pltpu.VMEM((2, TOP_K * tm, D // 2), jnp.int32),
                        pltpu.SemaphoreType.DMA((2,))],
    )
    return pl.pallas_call(
        _combine_kernel,
        grid_spec=grid_spec,
        out_shape=jax.ShapeDtypeStruct((nrows, D), F32),
        compiler_params=_cparams(("arbitrary",)),
        name="moe_combine",
    )(dest, yb, x_all, route, mods, ln_g, ln_b)


def _num_expert_blocks(n_tok):
    return -(-(n_tok * TOP_K + N_EXPERTS * (EXPERT_ROWS - 1)) // EXPERT_ROWS)


def _dispatch_indices(route, counts_row, n_tok):
    experts = jnp.arange(N_EXPERTS, dtype=jnp.int32)
    counts = counts_row[0, N_GROUPS:N_GROUPS + N_EXPERTS].astype(jnp.int32)
    padded = (counts + EXPERT_ROWS - 1) // EXPERT_ROWS * EXPERT_ROWS
    pad_end = jnp.cumsum(padded)
    pad_start = pad_end - padded
    n_used = (pad_end[-1] // EXPERT_ROWS).astype(jnp.int32)
    blk = jnp.arange(_num_expert_blocks(n_tok), dtype=jnp.int32)
    block_exp = jnp.sum((pad_end[None, :] <= (blk * EXPERT_ROWS)[:, None]).astype(jnp.int32), axis=1)
    last_exp = jnp.max(jnp.where(counts > 0, experts, 0))
    block_exp = jnp.where(blk < n_used, jnp.minimum(block_exp, N_EXPERTS - 1), last_exp).astype(jnp.int32)
    zrow = jnp.where(counts > 0, pad_end - EXPERT_ROWS, -1).astype(jnp.int32)
    e = route[:, 0:TOP_K].astype(jnp.int32)
    rank = route[:, 4:4 + TOP_K].astype(jnp.int32)
    base = jnp.sum(jnp.where(e[..., None] == experts, pad_start, 0), axis=-1)
    dest = (rank + base).astype(jnp.int32)
    dest_blocks = dest.reshape(n_tok // TM_COMB, TM_COMB, TOP_K).transpose(0, 2, 1).reshape(n_tok * TOP_K)
    return block_exp, n_used.reshape(1), dest_blocks, zrow


_IN_OFFS = dict(zip(("c_kv", "k_rot", "k_diff", "v_diff", "c_q", "q_diff", "pool", "gate"),
                    np.cumsum((0, MLA_KV_RANK, MLA_ROPE, 512, 512, MLA_Q_RANK, 512, POOL_W)).tolist()))
_IN_MOVES = ((_IN_OFFS["gate"], 3 * D, P_GATE), (_IN_OFFS["c_q"], MLA_Q_RANK, P_CQ),
             (_IN_OFFS["c_kv"], MLA_KV_RANK, P_CKV), (_IN_OFFS["k_diff"], 512, P_KD),
             (_IN_OFFS["v_diff"], 512, P_VD), (_IN_OFFS["q_diff"], 512, P_QD), (_IN_OFFS["pool"], POOL_W, P_POOL))
IN_COLS = _IN_OFFS["gate"] + 3 * D


def _relayout_in_kernel(w_ref, o_ref):
    for src, width, dst in _IN_MOVES:
        o_ref[:, dst:dst + width] = w_ref[:, src:src + width].astype(BF16)
    kr = w_ref[:, _IN_OFFS["k_rot"]:_IN_OFFS["k_rot"] + LANE]
    lane = lax.broadcasted_iota(jnp.int32, kr.shape, 1)
    o_ref[:, P_KROT:P_KROT + LANE] = jnp.where(lane < MLA_ROPE, kr, 0.0).astype(BF16)
    o_ref[:, P_KROT + LANE:P_CKV] = jnp.zeros((kr.shape[0], P_CKV - P_KROT - LANE), BF16)


def _relayout_w_in(w_in):
    tr = 256
    return pl.pallas_call(
        _relayout_in_kernel,
        grid=(DEPTH, D // tr),
        in_specs=[pl.BlockSpec((None, tr, IN_COLS), lambda l, i: (l, i, 0))],
        out_specs=pl.BlockSpec((None, tr, P_W), lambda l, i: (l, i, 0)),
        out_shape=jax.ShapeDtypeStruct((DEPTH, D, P_W), BF16),
        compiler_params=_cparams(("arbitrary", "arbitrary")),
        name="relayout_w_in",
    )(w_in)


def _relayout_w_uq(w_uq):
    w = w_uq.reshape(DEPTH, MLA_Q_RANK, MLA_HEADS, MLA_NOPE + MLA_ROPE)
    w = jnp.pad(w, ((0, 0), (0, 0), (0, 0), (0, MLA_HEAD_PAD - MLA_NOPE - MLA_ROPE)))
    return w.reshape(DEPTH, MLA_Q_RANK, MLA_HEADS * MLA_HEAD_PAD).astype(BF16)


def _relayout_w_ukv(w_ukv):
    w = w_ukv.reshape(DEPTH, MLA_KV_RANK, MLA_HEADS, MLA_NOPE + MLA_V)
    k = w[..., :MLA_NOPE].reshape(DEPTH, MLA_KV_RANK, MLA_HEADS * MLA_NOPE)
    v = w[..., MLA_NOPE:].reshape(DEPTH, MLA_KV_RANK, MLA_HEADS * MLA_V)
    return jnp.concatenate([k, v], axis=-1).astype(BF16)


def _relayout_router(w_grp, b_grp, w_exp, b_exp):
    w = jnp.concatenate([w_grp, w_exp], axis=-1)
    w = jnp.pad(w, ((0, 0), (0, 0), (0, ROUTE_LANES - w.shape[-1])))
    w_hi = w.astype(BF16)
    w_lo = (w - w_hi.astype(F32)).astype(BF16)
    b = jnp.pad(jnp.concatenate([b_grp, b_exp], axis=-1), ((0, 0), (0, ROUTE_LANES - N_GROUPS - N_EXPERTS)))
    return jnp.concatenate([w_hi, w_lo], axis=-1), b.reshape(DEPTH, 1, ROUTE_LANES)


def _rope_tables():
    quarter = MLA_ROPE // 4
    inv_freq = ROPE_BASE ** (-jnp.arange(quarter, dtype=F32) / quarter)
    pos = jnp.arange(SEQ)
    row = (pos // GRID_W).astype(F32)[:, None] * inv_freq
    col = (pos % GRID_W).astype(F32)[:, None] * inv_freq
    cos64 = jnp.concatenate([jnp.cos(row), jnp.cos(row), jnp.cos(col), jnp.cos(col)], axis=-1)
    sin64 = jnp.concatenate([-jnp.sin(row), jnp.sin(row), -jnp.sin(col), jnp.sin(col)], axis=-1)
    cos = jnp.concatenate([jnp.tile(cos64, (1, 2)), jnp.ones((TM_PREP, LANE), F32)], axis=0)
    sin = jnp.concatenate([jnp.tile(sin64, (1, 2)), jnp.zeros((TM_PREP, LANE), F32)], axis=0)
    return cos, sin


def kernel(x, c, ctx, c_ctx, w_ada, b_ada, w_in, b_gate, g_q, g_kv, w_uq, w_ukv, lam, g_sub, w_pool, pool_scale,
           w_br_mla, w_br_diff, w_br_pool, w_o, ln1_g, ln1_b, w_grp, b_grp, w_exp, b_exp, w_gu, w_dn, ln2_g, ln2_b):
    assert x.shape == (BATCH, SEQ, D) and ctx.shape == (BATCH, CTX, D)
    assert MLA_ROPE == DIFF_HD
    cc = jnp.concatenate([c, c_ctx[None], jnp.zeros((MOD_ROWS - BATCH - 1, D), F32)], axis=0)
    ada = _adaln(cc, w_ada, b_ada)
    w_in_p = _relayout_w_in(w_in)
    w_uq_p = _relayout_w_uq(w_uq)
    w_ukv_p = _relayout_w_ukv(w_ukv)
    w_route, b_route = _relayout_router(w_grp, b_grp, w_exp, b_exp)
    rope_cos, rope_sin = _rope_tables()
    bf = lambda a: a.astype(BF16)
    w_pool_b, w_br_mla_b, w_br_diff_b, w_br_pool_b, w_o_b = map(bf, (w_pool, w_br_mla, w_br_diff, w_br_pool, w_o))

    x_all = jnp.concatenate([x.reshape(N_LAT, D), ctx.reshape(N_CTX, D)], axis=0)
    for l in range(DEPTH):
        last = l == DEPTH - 1
        lam_init = 0.8 - 0.6 * math.exp(-0.3 * l)
        mods = ada[l].reshape(MOD_ROWS, 1, 6 * D)
        n_out = N_LAT if last else N_ALL
        g_kv_l, g_q_l = g_kv[l].reshape(1, -1), g_q[l].reshape(1, -1)
        g_sub_l, ps_l = g_sub[l].reshape(1, -1), pool_scale[l].reshape(1, -1)

        p = _proj(x_all, mods, w_in_p, l, 0, n_out, 0, P_W)
        k_mla, v_mla, k_diff = _kvprep(p, 0, 0, n_out, rope_cos, rope_sin, g_kv_l, w_ukv_p, l)
        q_mla, q_diff = _qprep(p, 0, n_out, rope_cos, rope_sin, g_q_l, w_uq_p, l)
        if last:
            p_c = _proj(x_all, mods, w_in_p, l, N_LAT, N_CTX, P_KV0, P_KV_W)
            k_mla_c, v_mla_c, k_diff_c = _kvprep(p_c, P_KV0, N_LAT, N_CTX, rope_cos, rope_sin, g_kv_l, w_ukv_p, l)
            c_row0, pv_c, pv_c_col = 0, p_c, (P_VD - P_KV0) // LANE
        else:
            k_mla_c, v_mla_c, k_diff_c, c_row0, pv_c, pv_c_col = k_mla, v_mla, k_diff, N_LAT, p, P_VD // LANE
        mla_segs = [(k_mla, 0, v_mla, 0, 0, SEQ), (k_mla_c, 0, v_mla_c, 0, c_row0, CTX)]
        diff_segs = [(k_diff, 0, p, P_VD // LANE, 0, SEQ), (k_diff_c, 0, pv_c, pv_c_col, c_row0, CTX)]
        diff_kw = dict(lam_init=lam_init, lam=lam[l], g_sub=g_sub_l)
        o_mla = _attention("mla", q_mla, 0, 0, SEQ, mla_segs)
        o_diff = _attention("diff", q_diff, 0, 0, SEQ, diff_segs, **diff_kw)
        o_pool = _pool(p, 0, SEQ, w_pool_b, ps_l, l)
        if not last:
            o_mla = jnp.concatenate([o_mla, _attention("mla", q_mla, 0, N_LAT, CTX, mla_segs[1:])], axis=0)
            o_diff = jnp.concatenate([o_diff, _attention("diff", q_diff, 0, N_LAT, CTX, diff_segs[1:], **diff_kw)], axis=0)
            o_pool = jnp.concatenate([o_pool, _pool(p, N_LAT, CTX, w_pool_b, ps_l, l)], axis=0)
        x_all = _merge(x_all, p, o_mla, o_diff, o_pool, mods, b_gate[l], w_br_mla_b, w_br_diff_b,
                       w_br_pool_b, w_o_b, ln1_g[l].reshape(1, D), ln1_b[l].reshape(1, D), l, n_out)

        vp, route, counts = _route(x_all, mods, w_route[l], b_route[l], n_out)
        block_exp, n_used, dest, zrow = _dispatch_indices(route, counts, n_out)
        xs = _dispatch(vp, dest, zrow, n_used, _num_expert_blocks(n_out) * EXPERT_ROWS)
        yb = _expert_ffn(xs, block_exp, n_used, w_gu, w_dn, l)
        x_all = _combine(x_all, yb, dest, route, mods, ln2_g[l].reshape(1, D), ln2_b[l].reshape(1, D), n_out)
    return x_all.reshape(BATCH, SEQ, D)
```

```python
import functools
import math

import jax
import jax.numpy as jnp
import numpy as np
from jax import lax
from jax.experimental import pallas as pl
from jax.experimental.pallas import tpu as pltpu

F32 = jnp.float32
BF16 = jnp.bfloat16

D = 2048
BATCH = 4
SEQ = 2048
CTX = 256
DEPTH = 2
GRID_W = 64
N_LAT = BATCH * SEQ
N_CTX = BATCH * CTX
N_ALL = N_LAT + N_CTX
MLA_HEADS = 8
MLA_Q_RANK = 768
MLA_KV_RANK = 512
MLA_NOPE = 128
MLA_ROPE = 64
MLA_V = 128
MLA_HEAD_PAD = 256
DIFF_HEADS = 4
DIFF_HD = 64
DIFF_V = 128
POOL_WINDOWS = (2, 4, 8, 16)
POOL_GROUP = 128
POOL_W = 512
N_GROUPS = 8
EXPERTS_PER_GROUP = 8
N_EXPERTS = 64
TOP_K = 2
D_EXPERT = 512
ROPE_BASE = 10000.0
EPS = 1e-6
DN_ALPHA = (2 * DEPTH) ** 0.25
MOD_ROWS = 8
CTX_MOD_ROW = BATCH

P_GATE = 0
P_CQ = 6144
P_KROT = 6912
P_CKV = 7168
P_KD = 7680
P_VD = 8192
P_QD = 8704
P_POOL = 9216
P_W = 9728
P_KV0 = 6656
P_KV_W = P_QD - P_KV0

LANE = 128
VMEM_LIMIT = 56 * 1024 * 1024

TM_PROJ = 1024
TN_PROJ = 512
TM_PREP = 512
TQ = 512
TQ_SUB = 256
TM_MERGE = 256
TM_ROUTE = 512
TM_COMB = 256
EXPERT_ROWS = 256
ROUTE_LANES = 128


def _cparams(sem, vmem=VMEM_LIMIT):
    return pltpu.CompilerParams(dimension_semantics=sem, vmem_limit_bytes=vmem)


def _mod_index(row_block, rows_per_block):
    return jnp.minimum((row_block * rows_per_block) // SEQ, CTX_MOD_ROW)


def _layer_norm_rows(x):
    mu = jnp.mean(x, axis=-1, keepdims=True)
    xc = x - mu
    var = jnp.mean(xc * xc, axis=-1, keepdims=True)
    return xc * lax.rsqrt(var + EPS)


def _rms_rows(x):
    return x * lax.rsqrt(jnp.mean(x * x, axis=-1, keepdims=True) + EPS)


def _adaln_kernel(c_ref, w_ref, b_ref, o_ref):
    c = c_ref[...]
    s = (c * jax.nn.sigmoid(c)).astype(BF16)
    o_ref[...] = jnp.dot(s, w_ref[...].astype(BF16), preferred_element_type=F32) + b_ref[...]


def _adaln(cc, w_ada, b_ada):
    tn = 1024
    n = w_ada.shape[-1]
    return pl.pallas_call(
        _adaln_kernel,
        grid=(DEPTH, n // tn),
        in_specs=[pl.BlockSpec((MOD_ROWS, D), lambda l, j: (0, 0)),
                  pl.BlockSpec((None, D, tn), lambda l, j: (l, 0, j)),
                  pl.BlockSpec((None, 1, tn), lambda l, j: (l, 0, j))],
        out_specs=pl.BlockSpec((None, MOD_ROWS, tn), lambda l, j: (l, 0, j)),
        out_shape=jax.ShapeDtypeStruct((DEPTH, MOD_ROWS, n), F32),
        compiler_params=_cparams(("arbitrary", "arbitrary")),
        name="adaln",
    )(cc, w_ada, b_ada.reshape(DEPTH, 1, n))


def _proj_kernel(x_ref, sh_ref, sc_ref, w_ref, o_ref, u_ref):
    @pl.when(pl.program_id(1) == 0)
    def _():
        u = _layer_norm_rows(x_ref[...]) * (1.0 + sc_ref[0]) + sh_ref[0]
        u_ref[...] = u.astype(BF16)

    o_ref[...] = jnp.dot(u_ref[...], w_ref[...], preferred_element_type=F32).astype(o_ref.dtype)


def _proj(x_all, mods, w, l, row0, nrows, col0, ncols):
    tm = min(TM_PROJ, nrows)
    rb0, cb0 = row0 // tm, col0 // TN_PROJ
    return pl.pallas_call(
        _proj_kernel,
        grid=(nrows // tm, ncols // TN_PROJ),
        in_specs=[pl.BlockSpec((tm, D), lambda i, j: (rb0 + i, 0)),
                  pl.BlockSpec((1, 1, D), lambda i, j: (_mod_index(rb0 + i, tm), 0, 0)),
                  pl.BlockSpec((1, 1, D), lambda i, j: (_mod_index(rb0 + i, tm), 0, 1)),
                  pl.BlockSpec((None, D, TN_PROJ), lambda i, j: (l, 0, cb0 + j))],
        out_specs=pl.BlockSpec((tm, TN_PROJ), lambda i, j: (i, j)),
        out_shape=jax.ShapeDtypeStruct((nrows, ncols), BF16),
        scratch_shapes=[pltpu.VMEM((tm, D), BF16)],
        compiler_params=_cparams(("arbitrary", "arbitrary")),
        name="proj_in",
    )(x_all, mods, mods, w)


def _rope128(x, cos, sin):
    lane = lax.broadcasted_iota(jnp.int32, x.shape, 1)
    nxt = pltpu.roll(x, LANE - MLA_ROPE // 4, 1)
    prv = pltpu.roll(x, MLA_ROPE // 4, 1)
    partner = jnp.where((lane % (MLA_ROPE // 2)) < MLA_ROPE // 4, nxt, prv)
    return x * cos + partner * sin


def _kvprep_kernel(ckv_ref, kd_ref, kr_ref, cos_ref, sin_ref, g_ref, w_ref, k_ref, v_ref, kdo_ref):
    cos, sin = cos_ref[...], sin_ref[...]
    c = (_rms_rows(ckv_ref[...].astype(F32)) * g_ref[...]).astype(BF16)
    kv = jnp.dot(c, w_ref[...], preferred_element_type=F32)
    krot = _rope128(kr_ref[...].astype(F32), cos, sin).astype(BF16)
    for h in range(MLA_HEADS):
        k_ref[:, h * MLA_HEAD_PAD:h * MLA_HEAD_PAD + MLA_NOPE] = kv[:, h * MLA_NOPE:(h + 1) * MLA_NOPE].astype(BF16)
        k_ref[:, h * MLA_HEAD_PAD + MLA_NOPE:(h + 1) * MLA_HEAD_PAD] = krot
    v_ref[...] = kv[:, MLA_HEADS * MLA_NOPE:].astype(BF16)
    for h in range(DIFF_HEADS):
        sl = slice(h * LANE, (h + 1) * LANE)
        kdo_ref[:, sl] = _rope128(kd_ref[:, sl].astype(F32), cos, sin).astype(BF16)


def _rope_block_index(i, tm, row0):
    r = row0 + i * tm
    return jnp.where(r < N_LAT, (r % SEQ) // tm, SEQ // tm)


def _kvprep(p, pcol0, row0, nrows, rope_cos, rope_sin, g_kv, w_ukv, l):
    tm = min(TM_PREP, nrows)
    rb0 = row0 // tm
    return pl.pallas_call(
        _kvprep_kernel,
        grid=(nrows // tm,),
        in_specs=[pl.BlockSpec((tm, MLA_KV_RANK), lambda i: (i, (P_CKV - pcol0) // MLA_KV_RANK)),
                  pl.BlockSpec((tm, 512), lambda i: (i, (P_KD - pcol0) // 512)),
                  pl.BlockSpec((tm, LANE), lambda i: (i, (P_KROT - pcol0) // LANE)),
                  pl.BlockSpec((tm, LANE), lambda i: (_rope_block_index(i, tm, row0), 0)),
                  pl.BlockSpec((tm, LANE), lambda i: (_rope_block_index(i, tm, row0), 0)),
                  pl.BlockSpec((1, MLA_KV_RANK), lambda i: (0, 0)),
                  pl.BlockSpec((None, MLA_KV_RANK, 2 * MLA_HEADS * MLA_NOPE), lambda i: (l, 0, 0))],
        out_specs=[pl.BlockSpec((tm, MLA_HEADS * MLA_HEAD_PAD), lambda i: (i, 0)),
                   pl.BlockSpec((tm, MLA_HEADS * MLA_V), lambda i: (i, 0)),
                   pl.BlockSpec((tm, 512), lambda i: (i, 0))],
        out_shape=[jax.ShapeDtypeStruct((nrows, MLA_HEADS * MLA_HEAD_PAD), BF16),
                   jax.ShapeDtypeStruct((nrows, MLA_HEADS * MLA_V), BF16),
                   jax.ShapeDtypeStruct((nrows, 512), BF16)],
        compiler_params=_cparams(("arbitrary",)),
        name="kv_prep",
    )(p, p, p, rope_cos, rope_sin, g_kv, w_ukv)


def _qprep_kernel(cq_ref, qd_ref, cos_ref, sin_ref, g_ref, w_ref, q_ref, qdo_ref):
    cos, sin = cos_ref[...], sin_ref[...]
    c = (_rms_rows(cq_ref[...].astype(F32)) * g_ref[...]).astype(BF16)
    q = jnp.dot(c, w_ref[...], preferred_element_type=F32)
    scale = (MLA_NOPE + MLA_ROPE) ** -0.5
    for h in range(MLA_HEADS):
        lo = h * MLA_HEAD_PAD
        q_ref[:, lo:lo + MLA_NOPE] = (q[:, lo:lo + MLA_NOPE] * scale).astype(BF16)
        q_ref[:, lo + MLA_NOPE:lo + MLA_HEAD_PAD] = (
            _rope128(q[:, lo + MLA_NOPE:lo + MLA_HEAD_PAD], cos, sin) * scale).astype(BF16)
    dscale = DIFF_HD ** -0.5
    for h in range(DIFF_HEADS):
        sl = slice(h * LANE, (h + 1) * LANE)
        qdo_ref[:, sl] = (_rope128(qd_ref[:, sl].astype(F32), cos, sin) * dscale).astype(BF16)


def _qprep(p, row0, nrows, rope_cos, rope_sin, g_q, w_uq, l):
    tm = min(TM_PREP, nrows)
    return pl.pallas_call(
        _qprep_kernel,
        grid=(nrows // tm,),
        in_specs=[pl.BlockSpec((tm, MLA_Q_RANK), lambda i: (i, P_CQ // MLA_Q_RANK)),
                  pl.BlockSpec((tm, 512), lambda i: (i, P_QD // 512)),
                  pl.BlockSpec((tm, LANE), lambda i: (_rope_block_index(i, tm, row0), 0)),
                  pl.BlockSpec((tm, LANE), lambda i: (_rope_block_index(i, tm, row0), 0)),
                  pl.BlockSpec((1, MLA_Q_RANK), lambda i: (0, 0)),
                  pl.BlockSpec((None, MLA_Q_RANK, MLA_HEADS * MLA_HEAD_PAD), lambda i: (l, 0, 0))],
        out_specs=[pl.BlockSpec((tm, MLA_HEADS * MLA_HEAD_PAD), lambda i: (i, 0)),
                   pl.BlockSpec((tm, 512), lambda i: (i, 0))],
        out_shape=[jax.ShapeDtypeStruct((nrows, MLA_HEADS * MLA_HEAD_PAD), BF16),
                   jax.ShapeDtypeStruct((nrows, 512), BF16)],
        compiler_params=_cparams(("arbitrary",)),
        name="q_prep",
    )(p, p, rope_cos, rope_sin, g_q, w_uq)


_NT = (((1,), (1,)), ((), ()))


def _softmax_parts(scores):
    m = scores[0].max(axis=-1, keepdims=True)
    for s in scores[1:]:
        m = jnp.maximum(m, s.max(axis=-1, keepdims=True))
    ps = [jnp.exp(s - m) for s in scores]
    tot = ps[0].sum(axis=-1, keepdims=True)
    for p in ps[1:]:
        tot = tot + p.sum(axis=-1, keepdims=True)
    return ps, 1.0 / tot


def _sub_tiles(n_rows):
    return [slice(r, r + TQ_SUB) for r in range(0, n_rows, TQ_SUB)]


def _mla_attn_kernel(n_seg, q_ref, *refs):
    k_refs, v_refs, o_ref = refs[:n_seg], refs[n_seg:2 * n_seg], refs[2 * n_seg]
    for rows in _sub_tiles(q_ref.shape[0]):
        q = q_ref[rows, :]
        scores = [lax.dot_general(q, k[...], _NT, preferred_element_type=F32) for k in k_refs]
        ps, inv = _softmax_parts(scores)
        acc = jnp.dot(ps[0].astype(BF16), v_refs[0][...], preferred_element_type=F32)
        for p, v in zip(ps[1:], v_refs[1:]):
            acc = acc + jnp.dot(p.astype(BF16), v[...], preferred_element_type=F32)
        o_ref[rows, :] = (acc * inv).astype(o_ref.dtype)


def _diff_attn_kernel(n_seg, lam_init, q_ref, lam_ref, g_ref, *refs):
    k_refs, v_refs, o_ref = refs[:n_seg], refs[n_seg:2 * n_seg], refs[2 * n_seg]
    lam = lam_ref[...]
    lam_val = (jnp.exp(jnp.sum(lam[0:1] * lam[1:2], axis=-1, keepdims=True))
               - jnp.exp(jnp.sum(lam[2:3] * lam[3:4], axis=-1, keepdims=True))) + lam_init
    for rows in _sub_tiles(q_ref.shape[0]):
        q = q_ref[rows, :]
        lane = lax.broadcasted_iota(jnp.int32, q.shape, 1)
        q1 = jnp.where(lane < DIFF_HD, q, jnp.zeros_like(q))
        q2 = jnp.where(lane < DIFF_HD, jnp.zeros_like(q), q)
        s1 = [lax.dot_general(q1, k[...], _NT, preferred_element_type=F32) for k in k_refs]
        s2 = [lax.dot_general(q2, k[...], _NT, preferred_element_type=F32) for k in k_refs]
        p1, inv1 = _softmax_parts(s1)
        p2, inv2 = _softmax_parts(s2)
        c2 = lam_val * inv2
        acc = None
        for a, b, v in zip(p1, p2, v_refs):
            w = (a * inv1 - b * c2).astype(BF16)
            t = jnp.dot(w, v[...], preferred_element_type=F32)
            acc = t if acc is None else acc + t
        o_ref[rows, :] = (_rms_rows(acc) * g_ref[...] * (1.0 - lam_init)).astype(o_ref.dtype)


def _attention(kind, q, q_col0, q_row0, n_q_per_batch, kv_segs, lam_init=None, lam=None, g_sub=None):
    heads, dq = (MLA_HEADS, MLA_HEAD_PAD) if kind == "mla" else (DIFF_HEADS, LANE)
    tq = min(TQ, n_q_per_batch)
    nq = n_q_per_batch // tq
    qb0 = q_row0 // tq
    qcb = q_col0 // dq
    n_seg = len(kv_segs)
    in_specs = [pl.BlockSpec((tq, dq), lambda b, h, i: (qb0 + b * nq + i, qcb + h))]
    args = [q]
    if kind == "diff":
        in_specs += [pl.BlockSpec((4, DIFF_HD), lambda b, h, i: (0, 0)),
                     pl.BlockSpec((1, DIFF_V), lambda b, h, i: (0, 0))]
        args += [lam, g_sub]
    for (k_arr, kcb, _, _, row0, rpb) in kv_segs:
        in_specs.append(pl.BlockSpec((rpb, dq), functools.partial(
            lambda b, h, i, r0, c0: (r0 + b, c0 + h), r0=row0 // rpb, c0=kcb)))
        args.append(k_arr)
    for (_, _, v_arr, vcb, row0, rpb) in kv_segs:
        in_specs.append(pl.BlockSpec((rpb, LANE), functools.partial(
            lambda b, h, i, r0, c0: (r0 + b, c0 + h), r0=row0 // rpb, c0=vcb)))
        args.append(v_arr)
    if kind == "mla":
        body = functools.partial(_mla_attn_kernel, n_seg)
    else:
        body = functools.partial(_diff_attn_kernel, n_seg, lam_init)
    return pl.pallas_call(
        body,
        grid=(BATCH, heads, nq),
        in_specs=in_specs,
        out_specs=pl.BlockSpec((tq, LANE), lambda b, h, i: (b * nq + i, h)),
        out_shape=jax.ShapeDtypeStruct((BATCH * n_q_per_batch, heads * LANE), BF16),
        compiler_params=_cparams(("arbitrary", "arbitrary", "arbitrary")),
        name=kind + "_attn",
    )(*args)


def _pool_kernel(seq, x_ref, w_ref, s_ref, o_ref):
    row = lax.broadcasted_iota(jnp.int32, (seq, POOL_GROUP), 0)

    def down(x, k):
        return jnp.where(row >= k, pltpu.roll(x, k, 0), 0.0)

    def up(x, k):
        return jnp.where(row < seq - k, pltpu.roll(x, seq - k, 0), 0.0)

    t = row.astype(F32)
    for gi, w in enumerate(POOL_WINDOWS):
        half = w // 2
        sl = slice(gi * POOL_GROUP, (gi + 1) * POOL_GROUP)
        x = x_ref[:, sl].astype(F32)
        before, after, span = down(x, 1), x, 1
        while span < half:
            before = before + down(before, span)
            after = after + up(after, span)
            span *= 2
        cnt = jnp.minimum(t + half, float(seq)) - jnp.maximum(t - half, 0.0)
        pooled = ((before + after) / cnt - x).astype(BF16)
        y = jnp.dot(pooled, w_ref[gi], preferred_element_type=F32)
        o_ref[:, sl] = (y * s_ref[:, sl]).astype(o_ref.dtype)


def _pool(p, row0, seq, w_pool, pool_scale, l):
    rb0 = row0 // seq
    return pl.pallas_call(
        functools.partial(_pool_kernel, seq),
        grid=(BATCH,),
        in_specs=[pl.BlockSpec((seq, POOL_W), lambda b: (rb0 + b, P_POOL // POOL_W)),
                  pl.BlockSpec((None, len(POOL_WINDOWS), POOL_GROUP, POOL_GROUP), lambda b: (l, 0, 0, 0)),
                  pl.BlockSpec((1, POOL_W), lambda b: (0, 0))],
        out_specs=pl.BlockSpec((seq, POOL_W), lambda b: (b, 0)),
        out_shape=jax.ShapeDtypeStruct((BATCH * seq, POOL_W), BF16),
        compiler_params=_cparams(("arbitrary",)),
        name="pool_mix",
    )(p, w_pool, pool_scale)


def _deepnorm(x, y, gate, g, b):
    return _layer_norm_rows(DN_ALPHA * x + gate * y) * g + b


def _merge_kernel(x_ref, gt_ref, om_ref, od_ref, op_ref, bg_ref, wm_ref, wd_ref, wp_ref, wo_ref,
                  g1_ref, lg_ref, lb_ref, o_ref):
    merged = None
    for br, (o_br, w_br) in enumerate(((om_ref, wm_ref), (od_ref, wd_ref), (op_ref, wp_ref))):
        y = jnp.dot(o_br[...], w_br[...], preferred_element_type=F32)
        gate = jax.nn.sigmoid(gt_ref[:, br * D:(br + 1) * D].astype(F32) + bg_ref[br:br + 1, :])
        merged = gate * y if merged is None else merged + gate * y
    out = jnp.dot(merged.astype(BF16), wo_ref[...], preferred_element_type=F32)
    o_ref[...] = _deepnorm(x_ref[...], out, g1_ref[0], lg_ref[...], lb_ref[...])


def _merge(x_all, p, o_mla, o_diff, o_pool, mods, b_gate, w_br_mla, w_br_diff, w_br_pool, w_o, ln_g, ln_b, l, nrows):
    tm = TM_MERGE
    const = lambda i: (0, 0)
    layer = lambda i: (l, 0, 0)
    once = dict(pipeline_mode=pl.Buffered(1))
    return pl.pallas_call(
        _merge_kernel,
        grid=(nrows // tm,),
        in_specs=[pl.BlockSpec((tm, D), lambda i: (i, 0)),
                  pl.BlockSpec((tm, 3 * D), lambda i: (i, P_GATE // (3 * D))),
                  pl.BlockSpec((tm, MLA_HEADS * MLA_V), lambda i: (i, 0)),
                  pl.BlockSpec((tm, DIFF_HEADS * DIFF_V), lambda i: (i, 0)),
                  pl.BlockSpec((tm, POOL_W), lambda i: (i, 0)),
                  pl.BlockSpec((3, D), const, **once),
                  pl.BlockSpec((None, MLA_HEADS * MLA_V, D), layer, **once),
                  pl.BlockSpec((None, DIFF_HEADS * DIFF_V, D), layer, **once),
                  pl.BlockSpec((None, POOL_W, D), layer, **once),
                  pl.BlockSpec((None, D, D), layer, **once),
                  pl.BlockSpec((1, 1, D), lambda i: (_mod_index(i, tm), 0, 2)),
                  pl.BlockSpec((1, D), const, **once),
                  pl.BlockSpec((1, D), const, **once)],
        out_specs=pl.BlockSpec((tm, D), lambda i: (i, 0)),
        out_shape=jax.ShapeDtypeStruct((nrows, D), F32),
        compiler_params=_cparams(("arbitrary",)),
        name="merge_out",
    )(x_all, p, o_mla, o_diff, o_pool, b_gate, w_br_mla, w_br_diff, w_br_pool, w_o, mods, ln_g, ln_b)


_HI_MASK = -65536


def _pack_bf16_pairs(lo, hi):
    lo_bits = lax.bitcast_convert_type(lo.astype(BF16).astype(F32), jnp.int32)
    hi_bits = lax.bitcast_convert_type(hi.astype(BF16).astype(F32), jnp.int32)
    return hi_bits | lax.shift_right_logical(lo_bits, jnp.full_like(lo_bits, 16))


def _unpack_bf16_pairs(w):
    lo = lax.bitcast_convert_type(jnp.left_shift(w, 16), F32)
    hi = lax.bitcast_convert_type(w & _HI_MASK, F32)
    return lo, hi


def _route_kernel(x_ref, sh_ref, sc_ref, w_ref, b_ref, vp_ref, r_ref, cnt_ref, carry_ref):
    @pl.when(pl.program_id(0) == 0)
    def _():
        carry_ref[...] = jnp.zeros_like(carry_ref)

    v = _layer_norm_rows(x_ref[...]) * (1.0 + sc_ref[0]) + sh_ref[0]
    vp_ref[...] = _pack_bf16_pairs(v[:, :D // 2], v[:, D // 2:])
    v_hi = v.astype(BF16)
    v_lo = (v - v_hi.astype(F32)).astype(BF16)
    t = jnp.dot(v_hi, w_ref[...], preferred_element_type=F32)
    logit = (t[:, :ROUTE_LANES] + t[:, ROUTE_LANES:]
             + jnp.dot(v_lo, w_ref[:, :ROUTE_LANES], preferred_element_type=F32) + b_ref[...])
    lane = lax.broadcasted_iota(jnp.int32, logit.shape, 1).astype(F32)
    neg = jnp.float32(-jnp.inf)
    big = jnp.float32(ROUTE_LANES)
    gl = jnp.where(lane < N_GROUPS, logit, neg)
    gmax = gl.max(axis=-1, keepdims=True)
    grp_p = 1.0 / jnp.exp(gl - gmax).sum(axis=-1, keepdims=True)
    grp = jnp.where(gl == gmax, lane, big).min(axis=-1, keepdims=True)
    grp_lo = N_GROUPS + EXPERTS_PER_GROUP * grp
    in_grp = (lane >= grp_lo) & (lane < grp_lo + EXPERTS_PER_GROUP)
    el = jnp.where(in_grp, logit, neg)
    t1 = el.max(axis=-1, keepdims=True)
    i1 = jnp.where(el == t1, lane, big).min(axis=-1, keepdims=True)
    el2 = jnp.where(lane == i1, neg, el)
    t2 = el2.max(axis=-1, keepdims=True)
    i2 = jnp.where(el2 == t2, lane, big).min(axis=-1, keepdims=True)
    e = jnp.exp(t2 - t1)
    w1 = grp_p / (1.0 + e)
    w2 = w1 * e
    oh1, oh2 = lane == i1, lane == i2
    oh = jnp.where(oh1 | oh2, 1.0, 0.0)
    tm = oh.shape[0]
    tri = (lax.broadcasted_iota(jnp.int32, (tm, tm), 1) < lax.broadcasted_iota(jnp.int32, (tm, tm), 0))
    before = jnp.dot(jnp.where(tri, 1.0, 0.0).astype(BF16), oh.astype(BF16),
                     preferred_element_type=F32) + carry_ref[...]
    rank1 = jnp.sum(jnp.where(oh1, before, 0.0), axis=-1, keepdims=True)
    rank2 = jnp.sum(jnp.where(oh2, before, 0.0), axis=-1, keepdims=True)
    carry_ref[...] += jnp.sum(oh, axis=0, keepdims=True)
    cnt_ref[...] = carry_ref[...]
    fields = (i1 - N_GROUPS, i2 - N_GROUPS, w1, w2, rank1, rank2)
    out = jnp.zeros_like(logit)
    for k, f in enumerate(fields):
        out = jnp.where(lane == k, f, out)
    r_ref[...] = out


def _route(x_all, mods, w_route, b_route, nrows):
    tm = TM_ROUTE
    return pl.pallas_call(
        _route_kernel,
        grid=(nrows // tm,),
        in_specs=[pl.BlockSpec((tm, D), lambda i: (i, 0)),
                  pl.BlockSpec((1, 1, D), lambda i: (_mod_index(i, tm), 0, 3)),
                  pl.BlockSpec((1, 1, D), lambda i: (_mod_index(i, tm), 0, 4)),
                  pl.BlockSpec((D, 2 * ROUTE_LANES), lambda i: (0, 0)),
                  pl.BlockSpec((1, ROUTE_LANES), lambda i: (0, 0))],
        out_specs=[pl.BlockSpec((tm, D // 2), lambda i: (i, 0)),
                   pl.BlockSpec((tm, ROUTE_LANES), lambda i: (i, 0)),
                   pl.BlockSpec((1, ROUTE_LANES), lambda i: (0, 0))],
        out_shape=[jax.ShapeDtypeStruct((nrows, D // 2), jnp.int32),
                   jax.ShapeDtypeStruct((nrows, ROUTE_LANES), F32),
                   jax.ShapeDtypeStruct((1, ROUTE_LANES), F32)],
        scratch_shapes=[pltpu.VMEM((1, ROUTE_LANES), F32)],
        compiler_params=_cparams(("arbitrary",)),
        name="moe_route",
    )(x_all, mods, mods, w_route, b_route)


def _gather_rows(src_hbm, idx_ref, base, n, dst, sem):
    def body(r, carry):
        tok = idx_ref[base + r]
        pltpu.make_async_copy(src_hbm.at[pl.ds(tok, 1), :], dst.at[pl.ds(r, 1), :], sem).start()
        return carry
    lax.fori_loop(0, n, body, 0, unroll=8)


def _wait_rows(src_hbm, n, dst, sem):
    pltpu.make_async_copy(src_hbm.at[pl.ds(0, n), :], dst, sem).wait()


def _dispatch_kernel(dest_ref, zrow_ref, nused_ref, vp_ref, xs_hbm, zbuf, sems):
    i = pl.program_id(0)
    tm = vp_ref.shape[0]
    row_sem, zero_sem = sems.at[0], sems.at[1]
    n_blocks = xs_hbm.shape[0] // EXPERT_ROWS

    @pl.when(i == 0)
    def _():
        zbuf[...] = jnp.zeros_like(zbuf)

        def clear(row):
            pltpu.make_async_copy(zbuf, xs_hbm.at[pl.ds(pl.multiple_of(row, EXPERT_ROWS), EXPERT_ROWS), :],
                                  zero_sem).start()

        def wait_clear():
            pltpu.make_async_copy(zbuf, xs_hbm.at[pl.ds(0, EXPERT_ROWS), :], zero_sem).wait()

        def fill(e, carry):
            @pl.when(zrow_ref[e] >= 0)
            def _():
                clear(zrow_ref[e])
            return carry
        lax.fori_loop(0, N_EXPERTS, fill, 0)

        def fill_tail(b, carry):
            clear(b * EXPERT_ROWS)
            return carry
        lax.fori_loop(nused_ref[0], n_blocks, fill_tail, 0)

        def drain(e, carry):
            @pl.when(zrow_ref[e] >= 0)
            def _():
                wait_clear()
            return carry
        lax.fori_loop(0, N_EXPERTS, drain, 0)

        def drain_tail(b, carry):
            wait_clear()
            return carry
        lax.fori_loop(nused_ref[0], n_blocks, drain_tail, 0)

    def scatter(r, carry):
        for k in range(TOP_K):
            slot = dest_ref[(i * TOP_K + k) * tm + r]
            pltpu.make_async_copy(vp_ref.at[pl.ds(r, 1), :], xs_hbm.at[pl.ds(slot, 1), :], row_sem).start()
        return carry
    lax.fori_loop(0, tm, scatter, 0, unroll=8)
    for _ in range(TOP_K):
        pltpu.make_async_copy(vp_ref, xs_hbm.at[pl.ds(0, tm), :], row_sem).wait()


def _dispatch(vp, dest, zrow, n_used, n_rows):
    tm = TM_COMB
    grid_spec = pltpu.PrefetchScalarGridSpec(
        num_scalar_prefetch=3,
        grid=(vp.shape[0] // tm,),
        in_specs=[pl.BlockSpec((tm, D // 2), lambda i, d, z, nu: (i, 0))],
        out_specs=pl.BlockSpec(memory_space=pl.ANY),
        scratch_shapes=[pltpu.VMEM((EXPERT_ROWS, D // 2), jnp.int32),
                        pltpu.SemaphoreType.DMA((2,))],
    )
    return pl.pallas_call(
        _dispatch_kernel,
        grid_spec=grid_spec,
        out_shape=jax.ShapeDtypeStruct((n_rows, D // 2), jnp.int32),
        compiler_params=_cparams(("arbitrary",)),
        name="moe_dispatch",
    )(dest, zrow, n_used, vp)


def _ffn_kernel(l, bexp_ref, nused_ref, next_ref, xs_ref, wgu_hbm, wdn_hbm, o_ref,
                wgu_f, wdn_f, wgu_bf, wdn_bf, slot_ref, sems):
    i = pl.program_id(0)
    n_used = nused_ref[0]
    e = bexp_ref[i]

    def weight_copies(expert, slot):
        return (pltpu.make_async_copy(wgu_hbm.at[l, expert], wgu_f.at[slot], sems.at[slot, 0]),
                pltpu.make_async_copy(wdn_hbm.at[l, expert], wdn_f.at[slot], sems.at[slot, 1]))

    @pl.when(i == 0)
    def _():
        slot_ref[0] = 0
        for cp in weight_copies(e, 0):
            cp.start()

    @pl.when((i == 0) | (e != bexp_ref[jnp.maximum(i - 1, 0)]))
    def _():
        slot = slot_ref[0]
        for cp in weight_copies(e, slot):
            cp.wait()
        nxt = next_ref[e]

        @pl.when(nxt >= 0)
        def _():
            for cp in weight_copies(nxt, 1 - slot):
                cp.start()

        wgu_bf[...] = wgu_f[slot].astype(BF16)
        wdn_bf[...] = wdn_f[slot].astype(BF16)
        slot_ref[0] = 1 - slot

    @pl.when(i < n_used)
    def _():
        lo, hi = _unpack_bf16_pairs(xs_ref[...])
        h = (jnp.dot(lo.astype(BF16), wgu_bf[:D // 2, :], preferred_element_type=F32)
             + jnp.dot(hi.astype(BF16), wgu_bf[D // 2:, :], preferred_element_type=F32))
        gate, up = h[:, :D_EXPERT], h[:, D_EXPERT:]
        a = (gate * jax.nn.sigmoid(gate) * up).astype(BF16)
        y = jnp.dot(a, wdn_bf[...], preferred_element_type=F32)
        o_ref[...] = _pack_bf16_pairs(y[:, :D // 2], y[:, D // 2:])

    @pl.when(i >= n_used)
    def _():
        o_ref[...] = jnp.zeros_like(o_ref)


def _expert_ffn(xs, block_exp, n_used, next_exp, w_gu, w_dn, l):
    n_blocks = block_exp.shape[0]
    grid_spec = pltpu.PrefetchScalarGridSpec(
        num_scalar_prefetch=3,
        grid=(n_blocks,),
        in_specs=[pl.BlockSpec((EXPERT_ROWS, D // 2), lambda i, be, nu, nx: (jnp.minimum(i, nu[0] - 1), 0)),
                  pl.BlockSpec(memory_space=pl.ANY),
                  pl.BlockSpec(memory_space=pl.ANY)],
        out_specs=pl.BlockSpec((EXPERT_ROWS, D // 2), lambda i, be, nu, nx: (i, 0)),
        scratch_shapes=[pltpu.VMEM((2, D, 2 * D_EXPERT), F32),
                        pltpu.VMEM((2, D_EXPERT, D), F32),
                        pltpu.VMEM((D, 2 * D_EXPERT), BF16),
                        pltpu.VMEM((D_EXPERT, D), BF16),
                        pltpu.SMEM((1,), jnp.int32),
                        pltpu.SemaphoreType.DMA((2, 2))],
    )
    return pl.pallas_call(
        functools.partial(_ffn_kernel, l),
        grid_spec=grid_spec,
        out_shape=jax.ShapeDtypeStruct((n_blocks * EXPERT_ROWS, D // 2), jnp.int32),
        compiler_params=_cparams(("arbitrary",)),
        name="expert_ffn",
    )(block_exp, n_used, next_exp, xs, w_gu, w_dn)


def _combine_kernel(dest_ref, y_hbm, x_ref, r_ref, g2_ref, lg_ref, lb_ref, o_ref, ybuf, sems):
    i = pl.program_id(0)
    n = pl.num_programs(0)
    tm = x_ref.shape[0]
    cur = i % 2

    @pl.when(i == 0)
    def _():
        _gather_rows(y_hbm, dest_ref, 0, TOP_K * tm, ybuf.at[0], sems.at[0])

    @pl.when(i + 1 < n)
    def _():
        _gather_rows(y_hbm, dest_ref, (i + 1) * TOP_K * tm, TOP_K * tm, ybuf.at[1 - cur], sems.at[1 - cur])

    _wait_rows(y_hbm, TOP_K * tm, ybuf.at[cur], sems.at[cur])
    r = r_ref[...]
    w1, w2 = r[:, 2:3], r[:, 3:4]
    lo1, hi1 = _unpack_bf16_pairs(ybuf[cur, 0:tm, :])
    lo2, hi2 = _unpack_bf16_pairs(ybuf[cur, tm:2 * tm, :])
    y = jnp.concatenate([lo1 * w1 + lo2 * w2, hi1 * w1 + hi2 * w2], axis=-1)
    o_ref[...] = _deepnorm(x_ref[...], y, g2_ref[0], lg_ref[...], lb_ref[...])


def _combine(x_all, yb, dest, route, mods, ln_g, ln_b, nrows):
    tm = TM_COMB
    grid_spec = pltpu.PrefetchScalarGridSpec(
        num_scalar_prefetch=1,
        grid=(nrows // tm,),
        in_specs=[pl.BlockSpec(memory_space=pl.ANY),
                  pl.BlockSpec((tm, D), lambda i, d: (i, 0)),
                  pl.BlockSpec((tm, ROUTE_LANES), lambda i, d: (i, 0)),
                  pl.BlockSpec((1, 1, D), lambda i, d: (_mod_index(i, tm), 0, 5)),
                  pl.BlockSpec((1, D), lambda i, d: (0, 0)),
                  pl.BlockSpec((1, D), lambda i, d: (0, 0))],
        out_specs=pl.BlockSpec((tm, D), lambda i, d: (i, 0)),
        scratch_shapes=[pltpu.VMEM((2, TOP_K * tm, D // 2), jnp.int32),
                        pltpu.SemaphoreType.DMA((2,))],
    )
    return pl.pallas_call(
        _combine_kernel,
        grid_spec=grid_spec,
        out_shape=jax.ShapeDtypeStruct((nrows, D), F32),
        compiler_params=_cparams(("arbitrary",)),
        name="moe_combine",
    )(dest, yb, x_all, route, mods, ln_g, ln_b)


def _num_expert_blocks(n_tok):
    return -(-(n_tok * TOP_K + N_EXPERTS * (EXPERT_ROWS - 1)) // EXPERT_ROWS)


def _dispatch_indices(route, counts_row, n_tok):
    experts = jnp.arange(N_EXPERTS, dtype=jnp.int32)
    counts = counts_row[0, N_GROUPS:N_GROUPS + N_EXPERTS].astype(jnp.int32)
    padded = (counts + EXPERT_ROWS - 1) // EXPERT_ROWS * EXPERT_ROWS
    pad_end = jnp.cumsum(padded)
    pad_start = pad_end - padded
    n_used = (pad_end[-1] // EXPERT_ROWS).astype(jnp.int32)
    blk = jnp.arange(_num_expert_blocks(n_tok), dtype=jnp.int32)
    block_exp = jnp.sum((pad_end[None, :] <= (blk * EXPERT_ROWS)[:, None]).astype(jnp.int32), axis=1)
    last_exp = jnp.max(jnp.where(counts > 0, experts, 0))
    block_exp = jnp.where(blk < n_used, jnp.minimum(block_exp, N_EXPERTS - 1), last_exp).astype(jnp.int32)
    zrow = jnp.where(counts > 0, pad_end - EXPERT_ROWS, -1).astype(jnp.int32)
    later = (experts[None, :] > experts[:, None]) & (counts > 0)[None, :]
    next_exp = jnp.min(jnp.where(later, experts[None, :], N_EXPERTS), axis=1)
    next_exp = jnp.where(next_exp < N_EXPERTS, next_exp, -1).astype(jnp.int32)
    e = route[:, 0:TOP_K].astype(jnp.int32)
    rank = route[:, 4:4 + TOP_K].astype(jnp.int32)
    base = jnp.sum(jnp.where(e[..., None] == experts, pad_start, 0), axis=-1)
    dest = (rank + base).astype(jnp.int32)
    dest_blocks = dest.reshape(n_tok // TM_COMB, TM_COMB, TOP_K).transpose(0, 2, 1).reshape(n_tok * TOP_K)
    return block_exp, n_used.reshape(1), dest_blocks, zrow, next_exp


_IN_OFFS = dict(zip(("c_kv", "k_rot", "k_diff", "v_diff", "c_q", "q_diff", "pool", "gate"),
                    np.cumsum((0, MLA_KV_RANK, MLA_ROPE, 512, 512, MLA_Q_RANK, 512, POOL_W)).tolist()))
_IN_MOVES = ((_IN_OFFS["gate"], 3 * D, P_GATE), (_IN_OFFS["c_q"], MLA_Q_RANK, P_CQ),
             (_IN_OFFS["c_kv"], MLA_KV_RANK, P_CKV), (_IN_OFFS["k_diff"], 512, P_KD),
             (_IN_OFFS["v_diff"], 512, P_VD), (_IN_OFFS["q_diff"], 512, P_QD), (_IN_OFFS["pool"], POOL_W, P_POOL))
IN_COLS = _IN_OFFS["gate"] + 3 * D


def _relayout_in_kernel(w_ref, o_ref):
    for src, width, dst in _IN_MOVES:
        o_ref[:, dst:dst + width] = w_ref[:, src:src + width].astype(BF16)
    kr = w_ref[:, _IN_OFFS["k_rot"]:_IN_OFFS["k_rot"] + LANE]
    lane = lax.broadcasted_iota(jnp.int32, kr.shape, 1)
    o_ref[:, P_KROT:P_KROT + LANE] = jnp.where(lane < MLA_ROPE, kr, 0.0).astype(BF16)
    o_ref[:, P_KROT + LANE:P_CKV] = jnp.zeros((kr.shape[0], P_CKV - P_KROT - LANE), BF16)


def _relayout_w_in(w_in):
    tr = 256
    return pl.pallas_call(
        _relayout_in_kernel,
        grid=(DEPTH, D // tr),
        in_specs=[pl.BlockSpec((None, tr, IN_COLS), lambda l, i: (l, i, 0))],
        out_specs=pl.BlockSpec((None, tr, P_W), lambda l, i: (l, i, 0)),
        out_shape=jax.ShapeDtypeStruct((DEPTH, D, P_W), BF16),
        compiler_params=_cparams(("arbitrary", "arbitrary")),
        name="relayout_w_in",
    )(w_in)


def _relayout_w_uq(w_uq):
    w = w_uq.reshape(DEPTH, MLA_Q_RANK, MLA_HEADS, MLA_NOPE + MLA_ROPE)
    w = jnp.pad(w, ((0, 0), (0, 0), (0, 0), (0, MLA_HEAD_PAD - MLA_NOPE - MLA_ROPE)))
    return w.reshape(DEPTH, MLA_Q_RANK, MLA_HEADS * MLA_HEAD_PAD).astype(BF16)


def _relayout_w_ukv(w_ukv):
    w = w_ukv.reshape(DEPTH, MLA_KV_RANK, MLA_HEADS, MLA_NOPE + MLA_V)
    k = w[..., :MLA_NOPE].reshape(DEPTH, MLA_KV_RANK, MLA_HEADS * MLA_NOPE)
    v = w[..., MLA_NOPE:].reshape(DEPTH, MLA_KV_RANK, MLA_HEADS * MLA_V)
    return jnp.concatenate([k, v], axis=-1).astype(BF16)


def _relayout_router(w_grp, b_grp, w_exp, b_exp):
    w = jnp.concatenate([w_grp, w_exp], axis=-1)
    w = jnp.pad(w, ((0, 0), (0, 0), (0, ROUTE_LANES - w.shape[-1])))
    w_hi = w.astype(BF16)
    w_lo = (w - w_hi.astype(F32)).astype(BF16)
    b = jnp.pad(jnp.concatenate([b_grp, b_exp], axis=-1), ((0, 0), (0, ROUTE_LANES - N_GROUPS - N_EXPERTS)))
    return jnp.concatenate([w_hi, w_lo], axis=-1), b.reshape(DEPTH, 1, ROUTE_LANES)


def _rope_tables():
    quarter = MLA_ROPE // 4
    inv_freq = ROPE_BASE ** (-jnp.arange(quarter, dtype=F32) / quarter)
    pos = jnp.arange(SEQ)
    row = (pos // GRID_W).astype(F32)[:, None] * inv_freq
    col = (pos % GRID_W).astype(F32)[:, None] * inv_freq
    cos64 = jnp.concatenate([jnp.cos(row), jnp.cos(row), jnp.cos(col), jnp.cos(col)], axis=-1)
    sin64 = jnp.concatenate([-jnp.sin(row), jnp.sin(row), -jnp.sin(col), jnp.sin(col)], axis=-1)
    cos = jnp.concatenate([jnp.tile(cos64, (1, 2)), jnp.ones((TM_PREP, LANE), F32)], axis=0)
    sin = jnp.concatenate([jnp.tile(sin64, (1, 2)), jnp.zeros((TM_PREP, LANE), F32)], axis=0)
    return cos, sin


def kernel(x, c, ctx, c_ctx, w_ada, b_ada, w_in, b_gate, g_q, g_kv, w_uq, w_ukv, lam, g_sub, w_pool, pool_scale,
           w_br_mla, w_br_diff, w_br_pool, w_o, ln1_g, ln1_b, w_grp, b_grp, w_exp, b_exp, w_gu, w_dn, ln2_g, ln2_b):
    assert x.shape == (BATCH, SEQ, D) and ctx.shape == (BATCH, CTX, D)
    assert MLA_ROPE == DIFF_HD
    cc = jnp.concatenate([c, c_ctx[None], jnp.zeros((MOD_ROWS - BATCH - 1, D), F32)], axis=0)
    ada = _adaln(cc, w_ada, b_ada)
    w_in_p = _relayout_w_in(w_in)
    w_uq_p = _relayout_w_uq(w_uq)
    w_ukv_p = _relayout_w_ukv(w_ukv)
    w_route, b_route = _relayout_router(w_grp, b_grp, w_exp, b_exp)
    rope_cos, rope_sin = _rope_tables()
    bf = lambda a: a.astype(BF16)
    w_pool_b, w_br_mla_b, w_br_diff_b, w_br_pool_b, w_o_b = map(bf, (w_pool, w_br_mla, w_br_diff, w_br_pool, w_o))

    x_all = jnp.concatenate([x.reshape(N_LAT, D), ctx.reshape(N_CTX, D)], axis=0)
    for l in range(DEPTH):
        last = l == DEPTH - 1
        lam_init = 0.8 - 0.6 * math.exp(-0.3 * l)
        mods = ada[l].reshape(MOD_ROWS, 1, 6 * D)
        n_out = N_LAT if last else N_ALL
        g_kv_l, g_q_l = g_kv[l].reshape(1, -1), g_q[l].reshape(1, -1)
        g_sub_l, ps_l = g_sub[l].reshape(1, -1), pool_scale[l].reshape(1, -1)

        p = _proj(x_all, mods, w_in_p, l, 0, n_out, 0, P_W)
        k_mla, v_mla, k_diff = _kvprep(p, 0, 0, n_out, rope_cos, rope_sin, g_kv_l, w_ukv_p, l)
        q_mla, q_diff = _qprep(p, 0, n_out, rope_cos, rope_sin, g_q_l, w_uq_p, l)
        if last:
            p_c = _proj(x_all, mods, w_in_p, l, N_LAT, N_CTX, P_KV0, P_KV_W)
            k_mla_c, v_mla_c, k_diff_c = _kvprep(p_c, P_KV0, N_LAT, N_CTX, rope_cos, rope_sin, g_kv_l, w_ukv_p, l)
            c_row0, pv_c, pv_c_col = 0, p_c, (P_VD - P_KV0) // LANE
        else:
            k_mla_c, v_mla_c, k_diff_c, c_row0, pv_c, pv_c_col = k_mla, v_mla, k_diff, N_LAT, p, P_VD // LANE
        mla_segs = [(k_mla, 0, v_mla, 0, 0, SEQ), (k_mla_c, 0, v_mla_c, 0, c_row0, CTX)]
        diff_segs = [(k_diff, 0, p, P_VD // LANE, 0, SEQ), (k_diff_c, 0, pv_c, pv_c_col, c_row0, CTX)]
        diff_kw = dict(lam_init=lam_init, lam=lam[l], g_sub=g_sub_l)
        o_mla = _attention("mla", q_mla, 0, 0, SEQ, mla_segs)
        o_diff = _attention("diff", q_diff, 0, 0, SEQ, diff_segs, **diff_kw)
        o_pool = _pool(p, 0, SEQ, w_pool_b, ps_l, l)
        if not last:
            o_mla = jnp.concatenate([o_mla, _attention("mla", q_mla, 0, N_LAT, CTX, mla_segs[1:])], axis=0)
            o_diff = jnp.concatenate([o_diff, _attention("diff", q_diff, 0, N_LAT, CTX, diff_segs[1:], **diff_kw)], axis=0)
            o_pool = jnp.concatenate([o_pool, _pool(p, N_LAT, CTX, w_pool_b, ps_l, l)], axis=0)
        x_all = _merge(x_all, p, o_mla, o_diff, o_pool, mods, b_gate[l], w_br_mla_b, w_br_diff_b,
                       w_br_pool_b, w_o_b, ln1_g[l].reshape(1, D), ln1_b[l].reshape(1, D), l, n_out)

        vp, route, counts = _route(x_all, mods, w_route[l], b_route[l], n_out)
        block_exp, n_used, dest, zrow, next_exp = _dispatch_indices(route, counts, n_out)
        xs = _dispatch(vp, dest, zrow, n_used, _num_expert_blocks(n_out) * EXPERT_ROWS)
        yb = _expert_ffn(xs, block_exp, n_used, next_exp, w_gu, w_dn, l)
        x_all = _combine(x_all, yb, dest, route, mods, ln2_g[l].reshape(1, D), ln2_b[l].reshape(1, D), n_out)
    return x_all.reshape(BATCH, SEQ, D)
```

```python
import functools
import math

import jax
import jax.numpy as jnp
import numpy as np
from jax import lax
from jax.experimental import pallas as pl
from jax.experimental.pallas import tpu as pltpu

F32 = jnp.float32
BF16 = jnp.bfloat16

D = 2048
BATCH = 4
SEQ = 2048
CTX = 256
DEPTH = 2
GRID_W = 64
N_LAT = BATCH * SEQ
N_CTX = BATCH * CTX
N_ALL = N_LAT + N_CTX
MLA_HEADS = 8
MLA_Q_RANK = 768
MLA_KV_RANK = 512
MLA_NOPE = 128
MLA_ROPE = 64
MLA_V = 128
MLA_HEAD_PAD = 256
DIFF_HEADS = 4
DIFF_HD = 64
DIFF_V = 128
POOL_WINDOWS = (2, 4, 8, 16)
POOL_GROUP = 128
POOL_W = 512
N_GROUPS = 8
EXPERTS_PER_GROUP = 8
N_EXPERTS = 64
TOP_K = 2
D_EXPERT = 512
ROPE_BASE = 10000.0
EPS = 1e-6
DN_ALPHA = (2 * DEPTH) ** 0.25
MOD_ROWS = 8
CTX_MOD_ROW = BATCH

P_GATE = 0
P_CQ = 6144
P_KROT = 6912
P_CKV = 7168
P_KD = 7680
P_VD = 8192
P_QD = 8704
P_POOL = 9216
P_W = 9728
P_KV0 = 6656
P_KV_W = P_QD - P_KV0

LANE = 128
VMEM_LIMIT = 56 * 1024 * 1024

TM_PROJ = 1024
TN_PROJ = 512
TM_PREP = 512
TQ = 1024
TQ_SUB = 256
TM_MERGE = 256
TM_ROUTE = 512
TM_COMB = 256
EXPERT_ROWS = 256
ROUTE_LANES = 128


def _cparams(sem, vmem=VMEM_LIMIT):
    return pltpu.CompilerParams(dimension_semantics=sem, vmem_limit_bytes=vmem)


def _mod_index(row_block, rows_per_block):
    return jnp.minimum((row_block * rows_per_block) // SEQ, CTX_MOD_ROW)


def _layer_norm_rows(x):
    mu = jnp.mean(x, axis=-1, keepdims=True)
    xc = x - mu
    var = jnp.mean(xc * xc, axis=-1, keepdims=True)
    return xc * lax.rsqrt(var + EPS)


def _rms_rows(x):
    return x * lax.rsqrt(jnp.mean(x * x, axis=-1, keepdims=True) + EPS)


def _adaln_kernel(c_ref, w_ref, b_ref, o_ref):
    c = c_ref[...]
    s = (c * jax.nn.sigmoid(c)).astype(BF16)
    o_ref[...] = jnp.dot(s, w_ref[...].astype(BF16), preferred_element_type=F32) + b_ref[...]


def _adaln(cc, w_ada, b_ada):
    tn = 1024
    n = w_ada.shape[-1]
    return pl.pallas_call(
        _adaln_kernel,
        grid=(DEPTH, n // tn),
        in_specs=[pl.BlockSpec((MOD_ROWS, D), lambda l, j: (0, 0)),
                  pl.BlockSpec((None, D, tn), lambda l, j: (l, 0, j)),
                  pl.BlockSpec((None, 1, tn), lambda l, j: (l, 0, j))],
        out_specs=pl.BlockSpec((None, MOD_ROWS, tn), lambda l, j: (l, 0, j)),
        out_shape=jax.ShapeDtypeStruct((DEPTH, MOD_ROWS, n), F32),
        compiler_params=_cparams(("arbitrary", "arbitrary")),
        name="adaln",
    )(cc, w_ada, b_ada.reshape(DEPTH, 1, n))


def _proj_kernel(x_ref, sh_ref, sc_ref, w_ref, o_ref, u_ref):
    @pl.when(pl.program_id(1) == 0)
    def _():
        u = _layer_norm_rows(x_ref[...]) * (1.0 + sc_ref[0]) + sh_ref[0]
        u_ref[...] = u.astype(BF16)

    o_ref[...] = jnp.dot(u_ref[...], w_ref[...], preferred_element_type=F32).astype(o_ref.dtype)


def _proj(x_all, mods, w, l, row0, nrows, col0, ncols):
    tm = min(TM_PROJ, nrows)
    rb0, cb0 = row0 // tm, col0 // TN_PROJ
    return pl.pallas_call(
        _proj_kernel,
        grid=(nrows // tm, ncols // TN_PROJ),
        in_specs=[pl.BlockSpec((tm, D), lambda i, j: (rb0 + i, 0)),
                  pl.BlockSpec((1, 1, D), lambda i, j: (_mod_index(rb0 + i, tm), 0, 0)),
                  pl.BlockSpec((1, 1, D), lambda i, j: (_mod_index(rb0 + i, tm), 0, 1)),
                  pl.BlockSpec((None, D, TN_PROJ), lambda i, j: (l, 0, cb0 + j))],
        out_specs=pl.BlockSpec((tm, TN_PROJ), lambda i, j: (i, j)),
        out_shape=jax.ShapeDtypeStruct((nrows, ncols), BF16),
        scratch_shapes=[pltpu.VMEM((tm, D), BF16)],
        compiler_params=_cparams(("arbitrary", "arbitrary")),
        name="proj_in",
    )(x_all, mods, mods, w)


def _rope128(x, cos, sin):
    lane = lax.broadcasted_iota(jnp.int32, x.shape, 1)
    nxt = pltpu.roll(x, LANE - MLA_ROPE // 4, 1)
    prv = pltpu.roll(x, MLA_ROPE // 4, 1)
    partner = jnp.where((lane % (MLA_ROPE // 2)) < MLA_ROPE // 4, nxt, prv)
    return x * cos + partner * sin


def _kvprep_kernel(ckv_ref, kd_ref, kr_ref, cos_ref, sin_ref, g_ref, w_ref, k_ref, v_ref, kdo_ref):
    cos, sin = cos_ref[...], sin_ref[...]
    c = (_rms_rows(ckv_ref[...].astype(F32)) * g_ref[...]).astype(BF16)
    kv = jnp.dot(c, w_ref[...], preferred_element_type=F32)
    krot = _rope128(kr_ref[...].astype(F32), cos, sin).astype(BF16)
    for h in range(MLA_HEADS):
        k_ref[:, h * MLA_HEAD_PAD:h * MLA_HEAD_PAD + MLA_NOPE] = kv[:, h * MLA_NOPE:(h + 1) * MLA_NOPE].astype(BF16)
        k_ref[:, h * MLA_HEAD_PAD + MLA_NOPE:(h + 1) * MLA_HEAD_PAD] = krot
    v_ref[...] = kv[:, MLA_HEADS * MLA_NOPE:].astype(BF16)
    for h in range(DIFF_HEADS):
        sl = slice(h * LANE, (h + 1) * LANE)
        kdo_ref[:, sl] = _rope128(kd_ref[:, sl].astype(F32), cos, sin).astype(BF16)


def _rope_block_index(i, tm, row0):
    r = row0 + i * tm
    return jnp.where(r < N_LAT, (r % SEQ) // tm, SEQ // tm)


def _kvprep(p, pcol0, row0, nrows, rope_cos, rope_sin, g_kv, w_ukv, l):
    tm = min(TM_PREP, nrows)
    rb0 = row0 // tm
    return pl.pallas_call(
        _kvprep_kernel,
        grid=(nrows // tm,),
        in_specs=[pl.BlockSpec((tm, MLA_KV_RANK), lambda i: (i, (P_CKV - pcol0) // MLA_KV_RANK)),
                  pl.BlockSpec((tm, 512), lambda i: (i, (P_KD - pcol0) // 512)),
                  pl.BlockSpec((tm, LANE), lambda i: (i, (P_KROT - pcol0) // LANE)),
                  pl.BlockSpec((tm, LANE), lambda i: (_rope_block_index(i, tm, row0), 0)),
                  pl.BlockSpec((tm, LANE), lambda i: (_rope_block_index(i, tm, row0), 0)),
                  pl.BlockSpec((1, MLA_KV_RANK), lambda i: (0, 0)),
                  pl.BlockSpec((None, MLA_KV_RANK, 2 * MLA_HEADS * MLA_NOPE), lambda i: (l, 0, 0))],
        out_specs=[pl.BlockSpec((tm, MLA_HEADS * MLA_HEAD_PAD), lambda i: (i, 0)),
                   pl.BlockSpec((tm, MLA_HEADS * MLA_V), lambda i: (i, 0)),
                   pl.BlockSpec((tm, 512), lambda i: (i, 0))],
        out_shape=[jax.ShapeDtypeStruct((nrows, MLA_HEADS * MLA_HEAD_PAD), BF16),
                   jax.ShapeDtypeStruct((nrows, MLA_HEADS * MLA_V), BF16),
                   jax.ShapeDtypeStruct((nrows, 512), BF16)],
        compiler_params=_cparams(("arbitrary",)),
        name="kv_prep",
    )(p, p, p, rope_cos, rope_sin, g_kv, w_ukv)


def _qprep_kernel(cq_ref, qd_ref, cos_ref, sin_ref, g_ref, w_ref, q_ref, qdo_ref):
    cos, sin = cos_ref[...], sin_ref[...]
    c = (_rms_rows(cq_ref[...].astype(F32)) * g_ref[...]).astype(BF16)
    q = jnp.dot(c, w_ref[...], preferred_element_type=F32)
    scale = (MLA_NOPE + MLA_ROPE) ** -0.5
    for h in range(MLA_HEADS):
        lo = h * MLA_HEAD_PAD
        q_ref[:, lo:lo + MLA_NOPE] = (q[:, lo:lo + MLA_NOPE] * scale).astype(BF16)
        q_ref[:, lo + MLA_NOPE:lo + MLA_HEAD_PAD] = (
            _rope128(q[:, lo + MLA_NOPE:lo + MLA_HEAD_PAD], cos, sin) * scale).astype(BF16)
    dscale = DIFF_HD ** -0.5
    for h in range(DIFF_HEADS):
        sl = slice(h * LANE, (h + 1) * LANE)
        qdo_ref[:, sl] = (_rope128(qd_ref[:, sl].astype(F32), cos, sin) * dscale).astype(BF16)


def _qprep(p, row0, nrows, rope_cos, rope_sin, g_q, w_uq, l):
    tm = min(TM_PREP, nrows)
    return pl.pallas_call(
        _qprep_kernel,
        grid=(nrows // tm,),
        in_specs=[pl.BlockSpec((tm, MLA_Q_RANK), lambda i: (i, P_CQ // MLA_Q_RANK)),
                  pl.BlockSpec((tm, 512), lambda i: (i, P_QD // 512)),
                  pl.BlockSpec((tm, LANE), lambda i: (_rope_block_index(i, tm, row0), 0)),
                  pl.BlockSpec((tm, LANE), lambda i: (_rope_block_index(i, tm, row0), 0)),
                  pl.BlockSpec((1, MLA_Q_RANK), lambda i: (0, 0)),
                  pl.BlockSpec((None, MLA_Q_RANK, MLA_HEADS * MLA_HEAD_PAD), lambda i: (l, 0, 0))],
        out_specs=[pl.BlockSpec((tm, MLA_HEADS * MLA_HEAD_PAD), lambda i: (i, 0)),
                   pl.BlockSpec((tm, 512), lambda i: (i, 0))],
        out_shape=[jax.ShapeDtypeStruct((nrows, MLA_HEADS * MLA_HEAD_PAD), BF16),
                   jax.ShapeDtypeStruct((nrows, 512), BF16)],
        compiler_params=_cparams(("arbitrary",)),
        name="q_prep",
    )(p, p, rope_cos, rope_sin, g_q, w_uq)


_NT = (((1,), (1,)), ((), ()))


def _softmax_parts(scores):
    m = scores[0].max(axis=-1, keepdims=True)
    for s in scores[1:]:
        m = jnp.maximum(m, s.max(axis=-1, keepdims=True))
    ps = [jnp.exp(s - m) for s in scores]
    tot = ps[0].sum(axis=-1, keepdims=True)
    for p in ps[1:]:
        tot = tot + p.sum(axis=-1, keepdims=True)
    return ps, 1.0 / tot


def _sub_tiles(n_rows):
    return [slice(r, r + TQ_SUB) for r in range(0, n_rows, TQ_SUB)]


def _mla_attn_kernel(n_seg, q_ref, *refs):
    k_refs, v_refs, o_ref = refs[:n_seg], refs[n_seg:2 * n_seg], refs[2 * n_seg]
    for rows in _sub_tiles(q_ref.shape[0]):
        q = q_ref[rows, :]
        scores = [lax.dot_general(q, k[...], _NT, preferred_element_type=F32) for k in k_refs]
        ps, inv = _softmax_parts(scores)
        acc = jnp.dot(ps[0].astype(BF16), v_refs[0][...], preferred_element_type=F32)
        for p, v in zip(ps[1:], v_refs[1:]):
            acc = acc + jnp.dot(p.astype(BF16), v[...], preferred_element_type=F32)
        o_ref[rows, :] = (acc * inv).astype(o_ref.dtype)


def _diff_attn_kernel(n_seg, lam_init, q_ref, lam_ref, g_ref, *refs):
    k_refs, v_refs, o_ref = refs[:n_seg], refs[n_seg:2 * n_seg], refs[2 * n_seg]
    lam = lam_ref[...]
    lam_val = (jnp.exp(jnp.sum(lam[0:1] * lam[1:2], axis=-1, keepdims=True))
               - jnp.exp(jnp.sum(lam[2:3] * lam[3:4], axis=-1, keepdims=True))) + lam_init
    for rows in _sub_tiles(q_ref.shape[0]):
        q = q_ref[rows, :]
        lane = lax.broadcasted_iota(jnp.int32, q.shape, 1)
        q1 = jnp.where(lane < DIFF_HD, q, jnp.zeros_like(q))
        q2 = jnp.where(lane < DIFF_HD, jnp.zeros_like(q), q)
        s1 = [lax.dot_general(q1, k[...], _NT, preferred_element_type=F32) for k in k_refs]
        s2 = [lax.dot_general(q2, k[...], _NT, preferred_element_type=F32) for k in k_refs]
        p1, inv1 = _softmax_parts(s1)
        p2, inv2 = _softmax_parts(s2)
        acc1 = acc2 = None
        for a, b, v in zip(p1, p2, v_refs):
            t1 = jnp.dot(a.astype(BF16), v[...], preferred_element_type=F32)
            t2 = jnp.dot(b.astype(BF16), v[...], preferred_element_type=F32)
            acc1, acc2 = (t1, t2) if acc1 is None else (acc1 + t1, acc2 + t2)
        acc = acc1 * inv1 - acc2 * (lam_val * inv2)
        o_ref[rows, :] = (_rms_rows(acc) * g_ref[...] * (1.0 - lam_init)).astype(o_ref.dtype)


def _attention(kind, q, q_col0, q_row0, n_q_per_batch, kv_segs, lam_init=None, lam=None, g_sub=None):
    heads, dq = (MLA_HEADS, MLA_HEAD_PAD) if kind == "mla" else (DIFF_HEADS, LANE)
    tq = min(TQ, n_q_per_batch)
    nq = n_q_per_batch // tq
    qb0 = q_row0 // tq
    qcb = q_col0 // dq
    n_seg = len(kv_segs)
    in_specs = [pl.BlockSpec((tq, dq), lambda b, h, i: (qb0 + b * nq + i, qcb + h))]
    args = [q]
    if kind == "diff":
        in_specs += [pl.BlockSpec((4, DIFF_HD), lambda b, h, i: (0, 0)),
                     pl.BlockSpec((1, DIFF_V), lambda b, h, i: (0, 0))]
        args += [lam, g_sub]
    for (k_arr, kcb, _, _, row0, rpb) in kv_segs:
        in_specs.append(pl.BlockSpec((rpb, dq), functools.partial(
            lambda b, h, i, r0, c0: (r0 + b, c0 + h), r0=row0 // rpb, c0=kcb)))
        args.append(k_arr)
    for (_, _, v_arr, vcb, row0, rpb) in kv_segs:
        in_specs.append(pl.BlockSpec((rpb, LANE), functools.partial(
            lambda b, h, i, r0, c0: (r0 + b, c0 + h), r0=row0 // rpb, c0=vcb)))
        args.append(v_arr)
    if kind == "mla":
        body = functools.partial(_mla_attn_kernel, n_seg)
    else:
        body = functools.partial(_diff_attn_kernel, n_seg, lam_init)
    return pl.pallas_call(
        body,
        grid=(BATCH, heads, nq),
        in_specs=in_specs,
        out_specs=pl.BlockSpec((tq, LANE), lambda b, h, i: (b * nq + i, h)),
        out_shape=jax.ShapeDtypeStruct((BATCH * n_q_per_batch, heads * LANE), BF16),
        compiler_params=_cparams(("arbitrary", "arbitrary", "arbitrary")),
        name=kind + "_attn",
    )(*args)


def _pool_kernel(seq, x_ref, w_ref, s_ref, o_ref):
    row = lax.broadcasted_iota(jnp.int32, (seq, POOL_GROUP), 0)

    def down(x, k):
        return jnp.where(row >= k, pltpu.roll(x, k, 0), 0.0)

    def up(x, k):
        return jnp.where(row < seq - k, pltpu.roll(x, seq - k, 0), 0.0)

    t = row.astype(F32)
    for gi, w in enumerate(POOL_WINDOWS):
        half = w // 2
        sl = slice(gi * POOL_GROUP, (gi + 1) * POOL_GROUP)
        x = x_ref[:, sl].astype(F32)
        before, after, span = down(x, 1), x, 1
        while span < half:
            before = before + down(before, span)
            after = after + up(after, span)
            span *= 2
        cnt = jnp.minimum(t + half, float(seq)) - jnp.maximum(t - half, 0.0)
        pooled = ((before + after) / cnt - x).astype(BF16)
        y = jnp.dot(pooled, w_ref[gi], preferred_element_type=F32)
        o_ref[:, sl] = (y * s_ref[:, sl]).astype(o_ref.dtype)


def _pool(p, row0, seq, w_pool, pool_scale, l):
    rb0 = row0 // seq
    return pl.pallas_call(
        functools.partial(_pool_kernel, seq),
        grid=(BATCH,),
        in_specs=[pl.BlockSpec((seq, POOL_W), lambda b: (rb0 + b, P_POOL // POOL_W)),
                  pl.BlockSpec((None, len(POOL_WINDOWS), POOL_GROUP, POOL_GROUP), lambda b: (l, 0, 0, 0)),
                  pl.BlockSpec((1, POOL_W), lambda b: (0, 0))],
        out_specs=pl.BlockSpec((seq, POOL_W), lambda b: (b, 0)),
        out_shape=jax.ShapeDtypeStruct((BATCH * seq, POOL_W), BF16),
        compiler_params=_cparams(("arbitrary",)),
        name="pool_mix",
    )(p, w_pool, pool_scale)


def _deepnorm(x, y, gate, g, b):
    return _layer_norm_rows(DN_ALPHA * x + gate * y) * g + b


def _merge_kernel(x_ref, gt_ref, om_ref, od_ref, op_ref, bg_ref, wm_ref, wd_ref, wp_ref, wo_ref,
                  g1_ref, lg_ref, lb_ref, o_ref):
    merged = None
    for br, (o_br, w_br) in enumerate(((om_ref, wm_ref), (od_ref, wd_ref), (op_ref, wp_ref))):
        y = jnp.dot(o_br[...], w_br[...], preferred_element_type=F32)
        gate = jax.nn.sigmoid(gt_ref[:, br * D:(br + 1) * D].astype(F32) + bg_ref[br:br + 1, :])
        merged = gate * y if merged is None else merged + gate * y
    out = jnp.dot(merged.astype(BF16), wo_ref[...], preferred_element_type=F32)
    o_ref[...] = _deepnorm(x_ref[...], out, g1_ref[0], lg_ref[...], lb_ref[...])


def _merge(x_all, p, o_mla, o_diff, o_pool, mods, b_gate, w_br_mla, w_br_diff, w_br_pool, w_o, ln_g, ln_b, l, nrows):
    tm = TM_MERGE
    const = lambda i: (0, 0)
    layer = lambda i: (l, 0, 0)
    once = dict(pipeline_mode=pl.Buffered(1))
    return pl.pallas_call(
        _merge_kernel,
        grid=(nrows // tm,),
        in_specs=[pl.BlockSpec((tm, D), lambda i: (i, 0)),
                  pl.BlockSpec((tm, 3 * D), lambda i: (i, P_GATE // (3 * D))),
                  pl.BlockSpec((tm, MLA_HEADS * MLA_V), lambda i: (i, 0)),
                  pl.BlockSpec((tm, DIFF_HEADS * DIFF_V), lambda i: (i, 0)),
                  pl.BlockSpec((tm, POOL_W), lambda i: (i, 0)),
                  pl.BlockSpec((3, D), const, **once),
                  pl.BlockSpec((None, MLA_HEADS * MLA_V, D), layer, **once),
                  pl.BlockSpec((None, DIFF_HEADS * DIFF_V, D), layer, **once),
                  pl.BlockSpec((None, POOL_W, D), layer, **once),
                  pl.BlockSpec((None, D, D), layer, **once),
                  pl.BlockSpec((1, 1, D), lambda i: (_mod_index(i, tm), 0, 2)),
                  pl.BlockSpec((1, D), const, **once),
                  pl.BlockSpec((1, D), const, **once)],
        out_specs=pl.BlockSpec((tm, D), lambda i: (i, 0)),
        out_shape=jax.ShapeDtypeStruct((nrows, D), F32),
        compiler_params=_cparams(("arbitrary",)),
        name="merge_out",
    )(x_all, p, o_mla, o_diff, o_pool, b_gate, w_br_mla, w_br_diff, w_br_pool, w_o, mods, ln_g, ln_b)


_HI_MASK = -65536


def _pack_bf16_pairs(lo, hi):
    lo_bits = lax.bitcast_convert_type(lo.astype(BF16).astype(F32), jnp.int32)
    hi_bits = lax.bitcast_convert_type(hi.astype(BF16).astype(F32), jnp.int32)
    return hi_bits | lax.shift_right_logical(lo_bits, jnp.full_like(lo_bits, 16))


def _unpack_bf16_pairs(w):
    lo = lax.bitcast_convert_type(jnp.left_shift(w, 16), F32)
    hi = lax.bitcast_convert_type(w & _HI_MASK, F32)
    return lo, hi


def _route_kernel(x_ref, sh_ref, sc_ref, w_ref, b_ref, vp_ref, r_ref, cnt_ref, carry_ref):
    @pl.when(pl.program_id(0) == 0)
    def _():
        carry_ref[...] = jnp.zeros_like(carry_ref)

    v = _layer_norm_rows(x_ref[...]) * (1.0 + sc_ref[0]) + sh_ref[0]
    vp_ref[...] = _pack_bf16_pairs(v[:, :D // 2], v[:, D // 2:])
    v_hi = v.astype(BF16)
    v_lo = (v - v_hi.astype(F32)).astype(BF16)
    t = jnp.dot(v_hi, w_ref[...], preferred_element_type=F32)
    logit = (t[:, :ROUTE_LANES] + t[:, ROUTE_LANES:]
             + jnp.dot(v_lo, w_ref[:, :ROUTE_LANES], preferred_element_type=F32) + b_ref[...])
    lane = lax.broadcasted_iota(jnp.int32, logit.shape, 1).astype(F32)
    neg = jnp.float32(-jnp.inf)
    big = jnp.float32(ROUTE_LANES)
    gl = jnp.where(lane < N_GROUPS, logit, neg)
    gmax = gl.max(axis=-1, keepdims=True)
    grp_p = 1.0 / jnp.exp(gl - gmax).sum(axis=-1, keepdims=True)
    grp = jnp.where(gl == gmax, lane, big).min(axis=-1, keepdims=True)
    grp_lo = N_GROUPS + EXPERTS_PER_GROUP * grp
    in_grp = (lane >= grp_lo) & (lane < grp_lo + EXPERTS_PER_GROUP)
    el = jnp.where(in_grp, logit, neg)
    t1 = el.max(axis=-1, keepdims=True)
    i1 = jnp.where(el == t1, lane, big).min(axis=-1, keepdims=True)
    el2 = jnp.where(lane == i1, neg, el)
    t2 = el2.max(axis=-1, keepdims=True)
    i2 = jnp.where(el2 == t2, lane, big).min(axis=-1, keepdims=True)
    e = jnp.exp(t2 - t1)
    w1 = grp_p / (1.0 + e)
    w2 = w1 * e
    oh1, oh2 = lane == i1, lane == i2
    oh = jnp.where(oh1 | oh2, 1.0, 0.0)
    tm = oh.shape[0]
    tri = (lax.broadcasted_iota(jnp.int32, (tm, tm), 1) < lax.broadcasted_iota(jnp.int32, (tm, tm), 0))
    before = jnp.dot(jnp.where(tri, 1.0, 0.0).astype(BF16), oh.astype(BF16),
                     preferred_element_type=F32) + carry_ref[...]
    rank1 = jnp.sum(jnp.where(oh1, before, 0.0), axis=-1, keepdims=True)
    rank2 = jnp.sum(jnp.where(oh2, before, 0.0), axis=-1, keepdims=True)
    carry_ref[...] += jnp.sum(oh, axis=0, keepdims=True)
    cnt_ref[...] = carry_ref[...]
    fields = (i1 - N_GROUPS, i2 - N_GROUPS, w1, w2, rank1, rank2)
    out = jnp.zeros_like(logit)
    for k, f in enumerate(fields):
        out = jnp.where(lane == k, f, out)
    r_ref[...] = out


def _route(x_all, mods, w_route, b_route, nrows):
    tm = TM_ROUTE
    return pl.pallas_call(
        _route_kernel,
        grid=(nrows // tm,),
        in_specs=[pl.BlockSpec((tm, D), lambda i: (i, 0)),
                  pl.BlockSpec((1, 1, D), lambda i: (_mod_index(i, tm), 0, 3)),
                  pl.BlockSpec((1, 1, D), lambda i: (_mod_index(i, tm), 0, 4)),
                  pl.BlockSpec((D, 2 * ROUTE_LANES), lambda i: (0, 0)),
                  pl.BlockSpec((1, ROUTE_LANES), lambda i: (0, 0))],
        out_specs=[pl.BlockSpec((tm, D // 2), lambda i: (i, 0)),
                   pl.BlockSpec((tm, ROUTE_LANES), lambda i: (i, 0)),
                   pl.BlockSpec((1, ROUTE_LANES), lambda i: (0, 0))],
        out_shape=[jax.ShapeDtypeStruct((nrows, D // 2), jnp.int32),
                   jax.ShapeDtypeStruct((nrows, ROUTE_LANES), F32),
                   jax.ShapeDtypeStruct((1, ROUTE_LANES), F32)],
        scratch_shapes=[pltpu.VMEM((1, ROUTE_LANES), F32)],
        compiler_params=_cparams(("arbitrary",)),
        name="moe_route",
    )(x_all, mods, mods, w_route, b_route)


def _gather_rows(src_hbm, idx_ref, base, n, dst, sem):
    def body(r, carry):
        tok = idx_ref[base + r]
        pltpu.make_async_copy(src_hbm.at[pl.ds(tok, 1), :], dst.at[pl.ds(r, 1), :], sem).start()
        return carry
    lax.fori_loop(0, n, body, 0, unroll=8)


def _wait_rows(src_hbm, n, dst, sem):
    pltpu.make_async_copy(src_hbm.at[pl.ds(0, n), :], dst, sem).wait()


def _dispatch_kernel(dest_ref, zrow_ref, nused_ref, vp_ref, xs_hbm, zbuf, sems):
    i = pl.program_id(0)
    tm = vp_ref.shape[0]
    row_sem, zero_sem = sems.at[0], sems.at[1]
    n_blocks = xs_hbm.shape[0] // EXPERT_ROWS

    @pl.when(i == 0)
    def _():
        zbuf[...] = jnp.zeros_like(zbuf)

        def clear(row):
            pltpu.make_async_copy(zbuf, xs_hbm.at[pl.ds(pl.multiple_of(row, EXPERT_ROWS), EXPERT_ROWS), :],
                                  zero_sem).start()

        def wait_clear():
            pltpu.make_async_copy(zbuf, xs_hbm.at[pl.ds(0, EXPERT_ROWS), :], zero_sem).wait()

        def fill(e, carry):
            @pl.when(zrow_ref[e] >= 0)
            def _():
                clear(zrow_ref[e])
            return carry
        lax.fori_loop(0, N_EXPERTS, fill, 0)

        def fill_tail(b, carry):
            clear(b * EXPERT_ROWS)
            return carry
        lax.fori_loop(nused_ref[0], n_blocks, fill_tail, 0)

        def drain(e, carry):
            @pl.when(zrow_ref[e] >= 0)
            def _():
                wait_clear()
            return carry
        lax.fori_loop(0, N_EXPERTS, drain, 0)

        def drain_tail(b, carry):
            wait_clear()
            return carry
        lax.fori_loop(nused_ref[0], n_blocks, drain_tail, 0)

    def scatter(r, carry):
        for k in range(TOP_K):
            slot = dest_ref[(i * TOP_K + k) * tm + r]
            pltpu.make_async_copy(vp_ref.at[pl.ds(r, 1), :], xs_hbm.at[pl.ds(slot, 1), :], row_sem).start()
        return carry
    lax.fori_loop(0, tm, scatter, 0, unroll=8)
    for _ in range(TOP_K):
        pltpu.make_async_copy(vp_ref, xs_hbm.at[pl.ds(0, tm), :], row_sem).wait()


def _dispatch(vp, dest, zrow, n_used, n_rows):
    tm = TM_COMB
    grid_spec = pltpu.PrefetchScalarGridSpec(
        num_scalar_prefetch=3,
        grid=(vp.shape[0] // tm,),
        in_specs=[pl.BlockSpec((tm, D // 2), lambda i, d, z, nu: (i, 0))],
        out_specs=pl.BlockSpec(memory_space=pl.ANY),
        scratch_shapes=[pltpu.VMEM((EXPERT_ROWS, D // 2), jnp.int32),
                        pltpu.SemaphoreType.DMA((2,))],
    )
    return pl.pallas_call(
        _dispatch_kernel,
        grid_spec=grid_spec,
        out_shape=jax.ShapeDtypeStruct((n_rows, D // 2), jnp.int32),
        compiler_params=_cparams(("arbitrary",)),
        name="moe_dispatch",
    )(dest, zrow, n_used, vp)


def _ffn_kernel(l, bexp_ref, nused_ref, next_ref, xs_ref, wgu_hbm, wdn_hbm, o_ref,
                wgu_f, wdn_f, wgu_bf, wdn_bf, slot_ref, sems):
    i = pl.program_id(0)
    n_used = nused_ref[0]
    e = bexp_ref[i]

    def weight_copies(expert, slot):
        return (pltpu.make_async_copy(wgu_hbm.at[l, expert], wgu_f.at[slot], sems.at[slot, 0]),
                pltpu.make_async_copy(wdn_hbm.at[l, expert], wdn_f.at[slot], sems.at[slot, 1]))

    @pl.when(i == 0)
    def _():
        slot_ref[0] = 0
        for cp in weight_copies(e, 0):
            cp.start()

    @pl.when((i == 0) | (e != bexp_ref[jnp.maximum(i - 1, 0)]))
    def _():
        slot = slot_ref[0]
        for cp in weight_copies(e, slot):
            cp.wait()
        nxt = next_ref[e]

        @pl.when(nxt >= 0)
        def _():
            for cp in weight_copies(nxt, 1 - slot):
                cp.start()

        wgu_bf[...] = wgu_f[slot].astype(BF16)
        wdn_bf[...] = wdn_f[slot].astype(BF16)
        slot_ref[0] = 1 - slot

    @pl.when(i < n_used)
    def _():
        lo, hi = _unpack_bf16_pairs(xs_ref[...])
        h = (jnp.dot(lo.astype(BF16), wgu_bf[:D // 2, :], preferred_element_type=F32)
             + jnp.dot(hi.astype(BF16), wgu_bf[D // 2:, :], preferred_element_type=F32))
        gate, up = h[:, :D_EXPERT], h[:, D_EXPERT:]
        a = (gate * jax.nn.sigmoid(gate) * up).astype(BF16)
        y = jnp.dot(a, wdn_bf[...], preferred_element_type=F32)
        o_ref[...] = _pack_bf16_pairs(y[:, :D // 2], y[:, D // 2:])

    @pl.when(i >= n_used)
    def _():
        o_ref[...] = jnp.zeros_like(o_ref)


def _expert_ffn(xs, block_exp, n_used, next_exp, w_gu, w_dn, l):
    n_blocks = block_exp.shape[0]
    grid_spec = pltpu.PrefetchScalarGridSpec(
        num_scalar_prefetch=3,
        grid=(n_blocks,),
        in_specs=[pl.BlockSpec((EXPERT_ROWS, D // 2), lambda i, be, nu, nx: (jnp.minimum(i, nu[0] - 1), 0)),
                  pl.BlockSpec(memory_space=pl.ANY),
                  pl.BlockSpec(memory_space=pl.ANY)],
        out_specs=pl.BlockSpec((EXPERT_ROWS, D // 2), lambda i, be, nu, nx: (i, 0)),
        scratch_shapes=[pltpu.VMEM((2, D, 2 * D_EXPERT), F32),
                        pltpu.VMEM((2, D_EXPERT, D), F32),
                        pltpu.VMEM((D, 2 * D_EXPERT), BF16),
                        pltpu.VMEM((D_EXPERT, D), BF16),
                        pltpu.SMEM((1,), jnp.int32),
                        pltpu.SemaphoreType.DMA((2, 2))],
    )
    return pl.pallas_call(
        functools.partial(_ffn_kernel, l),
        grid_spec=grid_spec,
        out_shape=jax.ShapeDtypeStruct((n_blocks * EXPERT_ROWS, D // 2), jnp.int32),
        compiler_params=_cparams(("arbitrary",)),
        name="expert_ffn",
    )(block_exp, n_used, next_exp, xs, w_gu, w_dn)


def _combine_kernel(dest_ref, y_hbm, x_ref, r_ref, g2_ref, lg_ref, lb_ref, o_ref, ybuf, sems):
    i = pl.program_id(0)
    n = pl.num_programs(0)
    tm = x_ref.shape[0]
    cur = i % 2

    @pl.when(i == 0)
    def _():
        _gather_rows(y_hbm, dest_ref, 0, TOP_K * tm, ybuf.at[0], sems.at[0])

    @pl.when(i + 1 < n)
    def _():
        _gather_rows(y_hbm, dest_ref, (i + 1) * TOP_K * tm, TOP_K * tm, ybuf.at[1 - cur], sems.at[1 - cur])

    _wait_rows(y_hbm, TOP_K * tm, ybuf.at[cur], sems.at[cur])
    r = r_ref[...]
    w1, w2 = r[:, 2:3], r[:, 3:4]
    lo1, hi1 = _unpack_bf16_pairs(ybuf[cur, 0:tm, :])
    lo2, hi2 = _unpack_bf16_pairs(ybuf[cur, tm:2 * tm, :])
    y = jnp.concatenate([lo1 * w1 + lo2 * w2, hi1 * w1 + hi2 * w2], axis=-1)
    o_ref[...] = _deepnorm(x_ref[...], y, g2_ref[0], lg_ref[...], lb_ref[...])


def _combine(x_all, yb, dest, route, mods, ln_g, ln_b, nrows):
    tm = TM_COMB
    grid_spec = pltpu.PrefetchScalarGridSpec(
        num_scalar_prefetch=1,
        grid=(nrows // tm,),
        in_specs=[pl.BlockSpec(memory_space=pl.ANY),
                  pl.BlockSpec((tm, D), lambda i, d: (i, 0)),
                  pl.BlockSpec((tm, ROUTE_LANES), lambda i, d: (i, 0)),
                  pl.BlockSpec((1, 1, D), lambda i, d: (_mod_index(i, tm), 0, 5)),
                  pl.BlockSpec((1, D), lambda i, d: (0, 0)),
                  pl.BlockSpec((1, D), lambda i, d: (0, 0))],
        out_specs=pl.BlockSpec((tm, D), lambda i, d: (i, 0)),
        scratch_shapes=[pltpu.VMEM((2, TOP_K * tm, D // 2), jnp.int32),
                        pltpu.SemaphoreType.DMA((2,))],
    )
    return pl.pallas_call(
        _combine_kernel,
        grid_spec=grid_spec,
        out_shape=jax.ShapeDtypeStruct((nrows, D), F32),
        compiler_params=_cparams(("arbitrary",)),
        name="moe_combine",
    )(dest, yb, x_all, route, mods, ln_g, ln_b)


def _num_expert_blocks(n_tok):
    return -(-(n_tok * TOP_K + N_EXPERTS * (EXPERT_ROWS - 1)) // EXPERT_ROWS)


def _dispatch_indices(route, counts_row, n_tok):
    experts = jnp.arange(N_EXPERTS, dtype=jnp.int32)
    counts = counts_row[0, N_GROUPS:N_GROUPS + N_EXPERTS].astype(jnp.int32)
    padded = (counts + EXPERT_ROWS - 1) // EXPERT_ROWS * EXPERT_ROWS
    pad_end = jnp.cumsum(padded)
    pad_start = pad_end - padded
    n_used = (pad_end[-1] // EXPERT_ROWS).astype(jnp.int32)
    blk = jnp.arange(_num_expert_blocks(n_tok), dtype=jnp.int32)
    block_exp = jnp.sum((pad_end[None, :] <= (blk * EXPERT_ROWS)[:, None]).astype(jnp.int32), axis=1)
    last_exp = jnp.max(jnp.where(counts > 0, experts, 0))
    block_exp = jnp.where(blk < n_used, jnp.minimum(block_exp, N_EXPERTS - 1), last_exp).astype(jnp.int32)
    zrow = jnp.where(counts > 0, pad_end - EXPERT_ROWS, -1).astype(jnp.int32)
    later = (experts[None, :] > experts[:, None]) & (counts > 0)[None, :]
    next_exp = jnp.min(jnp.where(later, experts[None, :], N_EXPERTS), axis=1)
    next_exp = jnp.where(next_exp < N_EXPERTS, next_exp, -1).astype(jnp.int32)
    e = route[:, 0:TOP_K].astype(jnp.int32)
    rank = route[:, 4:4 + TOP_K].astype(jnp.int32)
    base = jnp.sum(jnp.where(e[..., None] == experts, pad_start, 0), axis=-1)
    dest = (rank + base).astype(jnp.int32)
    dest_blocks = dest.reshape(n_tok // TM_COMB, TM_COMB, TOP_K).transpose(0, 2, 1).reshape(n_tok * TOP_K)
    return block_exp, n_used.reshape(1), dest_blocks, zrow, next_exp


_IN_OFFS = dict(zip(("c_kv", "k_rot", "k_diff", "v_diff", "c_q", "q_diff", "pool", "gate"),
                    np.cumsum((0, MLA_KV_RANK, MLA_ROPE, 512, 512, MLA_Q_RANK, 512, POOL_W)).tolist()))
_IN_MOVES = ((_IN_OFFS["gate"], 3 * D, P_GATE), (_IN_OFFS["c_q"], MLA_Q_RANK, P_CQ),
             (_IN_OFFS["c_kv"], MLA_KV_RANK, P_CKV), (_IN_OFFS["k_diff"], 512, P_KD),
             (_IN_OFFS["v_diff"], 512, P_VD), (_IN_OFFS["q_diff"], 512, P_QD), (_IN_OFFS["pool"], POOL_W, P_POOL))
IN_COLS = _IN_OFFS["gate"] + 3 * D


def _relayout_in_kernel(w_ref, o_ref):
    for src, width, dst in _IN_MOVES:
        o_ref[:, dst:dst + width] = w_ref[:, src:src + width].astype(BF16)
    kr = w_ref[:, _IN_OFFS["k_rot"]:_IN_OFFS["k_rot"] + LANE]
    lane = lax.broadcasted_iota(jnp.int32, kr.shape, 1)
    o_ref[:, P_KROT:P_KROT + LANE] = jnp.where(lane < MLA_ROPE, kr, 0.0).astype(BF16)
    o_ref[:, P_KROT + LANE:P_CKV] = jnp.zeros((kr.shape[0], P_CKV - P_KROT - LANE), BF16)


def _relayout_w_in(w_in):
    tr = 256
    return pl.pallas_call(
        _relayout_in_kernel,
        grid=(DEPTH, D // tr),
        in_specs=[pl.BlockSpec((None, tr, IN_COLS), lambda l, i: (l, i, 0))],
        out_specs=pl.BlockSpec((None, tr, P_W), lambda l, i: (l, i, 0)),
        out_shape=jax.ShapeDtypeStruct((DEPTH, D, P_W), BF16),
        compiler_params=_cparams(("arbitrary", "arbitrary")),
        name="relayout_w_in",
    )(w_in)


def _relayout_w_uq(w_uq):
    w = w_uq.reshape(DEPTH, MLA_Q_RANK, MLA_HEADS, MLA_NOPE + MLA_ROPE)
    w = jnp.pad(w, ((0, 0), (0, 0), (0, 0), (0, MLA_HEAD_PAD - MLA_NOPE - MLA_ROPE)))
    return w.reshape(DEPTH, MLA_Q_RANK, MLA_HEADS * MLA_HEAD_PAD).astype(BF16)


def _relayout_w_ukv(w_ukv):
    w = w_ukv.reshape(DEPTH, MLA_KV_RANK, MLA_HEADS, MLA_NOPE + MLA_V)
    k = w[..., :MLA_NOPE].reshape(DEPTH, MLA_KV_RANK, MLA_HEADS * MLA_NOPE)
    v = w[..., MLA_NOPE:].reshape(DEPTH, MLA_KV_RANK, MLA_HEADS * MLA_V)
    return jnp.concatenate([k, v], axis=-1).astype(BF16)


def _relayout_router(w_grp, b_grp, w_exp, b_exp):
    w = jnp.concatenate([w_grp, w_exp], axis=-1)
    w = jnp.pad(w, ((0, 0), (0, 0), (0, ROUTE_LANES - w.shape[-1])))
    w_hi = w.astype(BF16)
    w_lo = (w - w_hi.astype(F32)).astype(BF16)
    b = jnp.pad(jnp.concatenate([b_grp, b_exp], axis=-1), ((0, 0), (0, ROUTE_LANES - N_GROUPS - N_EXPERTS)))
    return jnp.concatenate([w_hi, w_lo], axis=-1), b.reshape(DEPTH, 1, ROUTE_LANES)


def _rope_tables():
    quarter = MLA_ROPE // 4
    inv_freq = ROPE_BASE ** (-jnp.arange(quarter, dtype=F32) / quarter)
    pos = jnp.arange(SEQ)
    row = (pos // GRID_W).astype(F32)[:, None] * inv_freq
    col = (pos % GRID_W).astype(F32)[:, None] * inv_freq
    cos64 = jnp.concatenate([jnp.cos(row), jnp.cos(row), jnp.cos(col), jnp.cos(col)], axis=-1)
    sin64 = jnp.concatenate([-jnp.sin(row), jnp.sin(row), -jnp.sin(col), jnp.sin(col)], axis=-1)
    cos = jnp.concatenate([jnp.tile(cos64, (1, 2)), jnp.ones((TM_PREP, LANE), F32)], axis=0)
    sin = jnp.concatenate([jnp.tile(sin64, (1, 2)), jnp.zeros((TM_PREP, LANE), F32)], axis=0)
    return cos, sin


def kernel(x, c, ctx, c_ctx, w_ada, b_ada, w_in, b_gate, g_q, g_kv, w_uq, w_ukv, lam, g_sub, w_pool, pool_scale,
           w_br_mla, w_br_diff, w_br_pool, w_o, ln1_g, ln1_b, w_grp, b_grp, w_exp, b_exp, w_gu, w_dn, ln2_g, ln2_b):
    assert x.shape == (BATCH, SEQ, D) and ctx.shape == (BATCH, CTX, D)
    assert MLA_ROPE == DIFF_HD
    cc = jnp.concatenate([c, c_ctx[None], jnp.zeros((MOD_ROWS - BATCH - 1, D), F32)], axis=0)
    ada = _adaln(cc, w_ada, b_ada)
    w_in_p = _relayout_w_in(w_in)
    w_uq_p = _relayout_w_uq(w_uq)
    w_ukv_p = _relayout_w_ukv(w_ukv)
    w_route, b_route = _relayout_router(w_grp, b_grp, w_exp, b_exp)
    rope_cos, rope_sin = _rope_tables()
    bf = lambda a: a.astype(BF16)
    w_pool_b, w_br_mla_b, w_br_diff_b, w_br_pool_b, w_o_b = map(bf, (w_pool, w_br_mla, w_br_diff, w_br_pool, w_o))

    x_all = jnp.concatenate([x.reshape(N_LAT, D), ctx.reshape(N_CTX, D)], axis=0)
    for l in range(DEPTH):
        last = l == DEPTH - 1
        lam_init = 0.8 - 0.6 * math.exp(-0.3 * l)
        mods = ada[l].reshape(MOD_ROWS, 1, 6 * D)
        n_out = N_LAT if last else N_ALL
        g_kv_l, g_q_l = g_kv[l].reshape(1, -1), g_q[l].reshape(1, -1)
        g_sub_l, ps_l = g_sub[l].reshape(1, -1), pool_scale[l].reshape(1, -1)

        p = _proj(x_all, mods, w_in_p, l, 0, n_out, 0, P_W)
        k_mla, v_mla, k_diff = _kvprep(p, 0, 0, n_out, rope_cos, rope_sin, g_kv_l, w_ukv_p, l)
        q_mla, q_diff = _qprep(p, 0, n_out, rope_cos, rope_sin, g_q_l, w_uq_p, l)
        if last:
            p_c = _proj(x_all, mods, w_in_p, l, N_LAT, N_CTX, P_KV0, P_KV_W)
            k_mla_c, v_mla_c, k_diff_c = _kvprep(p_c, P_KV0, N_LAT, N_CTX, rope_cos, rope_sin, g_kv_l, w_ukv_p, l)
            c_row0, pv_c, pv_c_col = 0, p_c, (P_VD - P_KV0) // LANE
        else:
            k_mla_c, v_mla_c, k_diff_c, c_row0, pv_c, pv_c_col = k_mla, v_mla, k_diff, N_LAT, p, P_VD // LANE
        mla_segs = [(k_mla, 0, v_mla, 0, 0, SEQ), (k_mla_c, 0, v_mla_c, 0, c_row0, CTX)]
        diff_segs = [(k_diff, 0, p, P_VD // LANE, 0, SEQ), (k_diff_c, 0, pv_c, pv_c_col, c_row0, CTX)]
        diff_kw = dict(lam_init=lam_init, lam=lam[l], g_sub=g_sub_l)
        o_mla = _attention("mla", q_mla, 0, 0, SEQ, mla_segs)
        o_diff = _attention("diff", q_diff, 0, 0, SEQ, diff_segs, **diff_kw)
        o_pool = _pool(p, 0, SEQ, w_pool_b, ps_l, l)
        if not last:
            o_mla = jnp.concatenate([o_mla, _attention("mla", q_mla, 0, N_LAT, CTX, mla_segs[1:])], axis=0)
            o_diff = jnp.concatenate([o_diff, _attention("diff", q_diff, 0, N_LAT, CTX, diff_segs[1:], **diff_kw)], axis=0)
            o_pool = jnp.concatenate([o_pool, _pool(p, N_LAT, CTX, w_pool_b, ps_l, l)], axis=0)
        x_all = _merge(x_all, p, o_mla, o_diff, o_pool, mods, b_gate[l], w_br_mla_b, w_br_diff_b,
                       w_br_pool_b, w_o_b, ln1_g[l].reshape(1, D), ln1_b[l].reshape(1, D), l, n_out)

        vp, route, counts = _route(x_all, mods, w_route[l], b_route[l], n_out)
        block_exp, n_used, dest, zrow, next_exp = _dispatch_indices(route, counts, n_out)
        xs = _dispatch(vp, dest, zrow, n_used, _num_expert_blocks(n_out) * EXPERT_ROWS)
        yb = _expert_ffn(xs, block_exp, n_used, next_exp, w_gu, w_dn, l)
        x_all = _combine(x_all, yb, dest, route, mods, ln2_g[l].reshape(1, D), ln2_b[l].reshape(1, D), n_out)
    return x_all.reshape(BATCH, SEQ, D)
```

```python
import functools
import math

import jax
import jax.numpy as jnp
import numpy as np
from jax import lax
from jax.experimental import pallas as pl
from jax.experimental.pallas import tpu as pltpu

F32 = jnp.float32
BF16 = jnp.bfloat16

D = 2048
BATCH = 4
SEQ = 2048
CTX = 256
DEPTH = 2
GRID_W = 64
N_LAT = BATCH * SEQ
N_CTX = BATCH * CTX
N_ALL = N_LAT + N_CTX
MLA_HEADS = 8
MLA_Q_RANK = 768
MLA_KV_RANK = 512
MLA_NOPE = 128
MLA_ROPE = 64
MLA_V = 128
MLA_HEAD_PAD = 256
DIFF_HEADS = 4
DIFF_HD = 64
DIFF_V = 128
POOL_WINDOWS = (2, 4, 8, 16)
POOL_GROUP = 128
POOL_W = 512
N_GROUPS = 8
EXPERTS_PER_GROUP = 8
N_EXPERTS = 64
TOP_K = 2
D_EXPERT = 512
ROPE_BASE = 10000.0
EPS = 1e-6
DN_ALPHA = (2 * DEPTH) ** 0.25
MOD_ROWS = 8
CTX_MOD_ROW = BATCH

P_GATE = 0
P_CQ = 6144
P_KROT = 6912
P_CKV = 7168
P_KD = 7680
P_VD = 8192
P_QD = 8704
P_POOL = 9216
P_W = 9728
P_KV0 = 6656
P_KV_W = P_QD - P_KV0

LANE = 128
VMEM_LIMIT = 56 * 1024 * 1024

TM_PROJ = 1024
TN_PROJ = 512
TM_PREP = 512
TQ = 2048
TQ_SUB = 256
TM_MERGE = 256
TM_ROUTE = 512
TM_COMB = 256
EXPERT_ROWS = 256
ROUTE_LANES = 128


def _cparams(sem, vmem=VMEM_LIMIT):
    return pltpu.CompilerParams(dimension_semantics=sem, vmem_limit_bytes=vmem)


def _mod_index(row_block, rows_per_block):
    return jnp.minimum((row_block * rows_per_block) // SEQ, CTX_MOD_ROW)


def _layer_norm_rows(x):
    mu = jnp.mean(x, axis=-1, keepdims=True)
    xc = x - mu
    var = jnp.mean(xc * xc, axis=-1, keepdims=True)
    return xc * lax.rsqrt(var + EPS)


def _rms_rows(x):
    return x * lax.rsqrt(jnp.mean(x * x, axis=-1, keepdims=True) + EPS)


def _adaln_kernel(c_ref, w_ref, b_ref, o_ref):
    c = c_ref[...]
    s = (c * jax.nn.sigmoid(c)).astype(BF16)
    o_ref[...] = jnp.dot(s, w_ref[...].astype(BF16), preferred_element_type=F32) + b_ref[...]


def _adaln(cc, w_ada, b_ada):
    tn = 1024
    n = w_ada.shape[-1]
    return pl.pallas_call(
        _adaln_kernel,
        grid=(DEPTH, n // tn),
        in_specs=[pl.BlockSpec((MOD_ROWS, D), lambda l, j: (0, 0)),
                  pl.BlockSpec((None, D, tn), lambda l, j: (l, 0, j)),
                  pl.BlockSpec((None, 1, tn), lambda l, j: (l, 0, j))],
        out_specs=pl.BlockSpec((None, MOD_ROWS, tn), lambda l, j: (l, 0, j)),
        out_shape=jax.ShapeDtypeStruct((DEPTH, MOD_ROWS, n), F32),
        compiler_params=_cparams(("arbitrary", "arbitrary")),
        name="adaln",
    )(cc, w_ada, b_ada.reshape(DEPTH, 1, n))


def _proj_kernel(x_ref, sh_ref, sc_ref, w_ref, o_ref, u_ref):
    @pl.when(pl.program_id(1) == 0)
    def _():
        u = _layer_norm_rows(x_ref[...]) * (1.0 + sc_ref[0]) + sh_ref[0]
        u_ref[...] = u.astype(BF16)

    o_ref[...] = jnp.dot(u_ref[...], w_ref[...], preferred_element_type=F32).astype(o_ref.dtype)


def _proj(x_all, mods, w, l, row0, nrows, col0, ncols):
    tm = min(TM_PROJ, nrows)
    rb0, cb0 = row0 // tm, col0 // TN_PROJ
    return pl.pallas_call(
        _proj_kernel,
        grid=(nrows // tm, ncols // TN_PROJ),
        in_specs=[pl.BlockSpec((tm, D), lambda i, j: (rb0 + i, 0)),
                  pl.BlockSpec((1, 1, D), lambda i, j: (_mod_index(rb0 + i, tm), 0, 0)),
                  pl.BlockSpec((1, 1, D), lambda i, j: (_mod_index(rb0 + i, tm), 0, 1)),
                  pl.BlockSpec((None, D, TN_PROJ), lambda i, j: (l, 0, cb0 + j))],
        out_specs=pl.BlockSpec((tm, TN_PROJ), lambda i, j: (i, j)),
        out_shape=jax.ShapeDtypeStruct((nrows, ncols), BF16),
        scratch_shapes=[pltpu.VMEM((tm, D), BF16)],
        compiler_params=_cparams(("arbitrary", "arbitrary")),
        name="proj_in",
    )(x_all, mods, mods, w)


def _rope128(x, cos, sin):
    lane = lax.broadcasted_iota(jnp.int32, x.shape, 1)
    nxt = pltpu.roll(x, LANE - MLA_ROPE // 4, 1)
    prv = pltpu.roll(x, MLA_ROPE // 4, 1)
    partner = jnp.where((lane % (MLA_ROPE // 2)) < MLA_ROPE // 4, nxt, prv)
    return x * cos + partner * sin


def _kvprep_kernel(ckv_ref, kd_ref, kr_ref, cos_ref, sin_ref, g_ref, w_ref, k_ref, v_ref, kdo_ref):
    cos, sin = cos_ref[...], sin_ref[...]
    c = (_rms_rows(ckv_ref[...].astype(F32)) * g_ref[...]).astype(BF16)
    kv = jnp.dot(c, w_ref[...], preferred_element_type=F32)
    krot = _rope128(kr_ref[...].astype(F32), cos, sin).astype(BF16)
    for h in range(MLA_HEADS):
        k_ref[:, h * MLA_HEAD_PAD:h * MLA_HEAD_PAD + MLA_NOPE] = kv[:, h * MLA_NOPE:(h + 1) * MLA_NOPE].astype(BF16)
        k_ref[:, h * MLA_HEAD_PAD + MLA_NOPE:(h + 1) * MLA_HEAD_PAD] = krot
    v_ref[...] = kv[:, MLA_HEADS * MLA_NOPE:].astype(BF16)
    for h in range(DIFF_HEADS):
        sl = slice(h * LANE, (h + 1) * LANE)
        kdo_ref[:, sl] = _rope128(kd_ref[:, sl].astype(F32), cos, sin).astype(BF16)


def _rope_block_index(i, tm, row0):
    r = row0 + i * tm
    return jnp.where(r < N_LAT, (r % SEQ) // tm, SEQ // tm)


def _kvprep(p, pcol0, row0, nrows, rope_cos, rope_sin, g_kv, w_ukv, l):
    tm = min(TM_PREP, nrows)
    rb0 = row0 // tm
    return pl.pallas_call(
        _kvprep_kernel,
        grid=(nrows // tm,),
        in_specs=[pl.BlockSpec((tm, MLA_KV_RANK), lambda i: (i, (P_CKV - pcol0) // MLA_KV_RANK)),
                  pl.BlockSpec((tm, 512), lambda i: (i, (P_KD - pcol0) // 512)),
                  pl.BlockSpec((tm, LANE), lambda i: (i, (P_KROT - pcol0) // LANE)),
                  pl.BlockSpec((tm, LANE), lambda i: (_rope_block_index(i, tm, row0), 0)),
                  pl.BlockSpec((tm, LANE), lambda i: (_rope_block_index(i, tm, row0), 0)),
                  pl.BlockSpec((1, MLA_KV_RANK), lambda i: (0, 0)),
                  pl.BlockSpec((None, MLA_KV_RANK, 2 * MLA_HEADS * MLA_NOPE), lambda i: (l, 0, 0))],
        out_specs=[pl.BlockSpec((tm, MLA_HEADS * MLA_HEAD_PAD), lambda i: (i, 0)),
                   pl.BlockSpec((tm, MLA_HEADS * MLA_V), lambda i: (i, 0)),
                   pl.BlockSpec((tm, 512), lambda i: (i, 0))],
        out_shape=[jax.ShapeDtypeStruct((nrows, MLA_HEADS * MLA_HEAD_PAD), BF16),
                   jax.ShapeDtypeStruct((nrows, MLA_HEADS * MLA_V), BF16),
                   jax.ShapeDtypeStruct((nrows, 512), BF16)],
        compiler_params=_cparams(("arbitrary",)),
        name="kv_prep",
    )(p, p, p, rope_cos, rope_sin, g_kv, w_ukv)


def _qprep_kernel(cq_ref, qd_ref, cos_ref, sin_ref, g_ref, w_ref, q_ref, qdo_ref):
    cos, sin = cos_ref[...], sin_ref[...]
    c = (_rms_rows(cq_ref[...].astype(F32)) * g_ref[...]).astype(BF16)
    q = jnp.dot(c, w_ref[...], preferred_element_type=F32)
    scale = (MLA_NOPE + MLA_ROPE) ** -0.5
    for h in range(MLA_HEADS):
        lo = h * MLA_HEAD_PAD
        q_ref[:, lo:lo + MLA_NOPE] = (q[:, lo:lo + MLA_NOPE] * scale).astype(BF16)
        q_ref[:, lo + MLA_NOPE:lo + MLA_HEAD_PAD] = (
            _rope128(q[:, lo + MLA_NOPE:lo + MLA_HEAD_PAD], cos, sin) * scale).astype(BF16)
    dscale = DIFF_HD ** -0.5
    for h in range(DIFF_HEADS):
        sl = slice(h * LANE, (h + 1) * LANE)
        qdo_ref[:, sl] = (_rope128(qd_ref[:, sl].astype(F32), cos, sin) * dscale).astype(BF16)


def _qprep(p, row0, nrows, rope_cos, rope_sin, g_q, w_uq, l):
    tm = min(TM_PREP, nrows)
    return pl.pallas_call(
        _qprep_kernel,
        grid=(nrows // tm,),
        in_specs=[pl.BlockSpec((tm, MLA_Q_RANK), lambda i: (i, P_CQ // MLA_Q_RANK)),
                  pl.BlockSpec((tm, 512), lambda i: (i, P_QD // 512)),
                  pl.BlockSpec((tm, LANE), lambda i: (_rope_block_index(i, tm, row0), 0)),
                  pl.BlockSpec((tm, LANE), lambda i: (_rope_block_index(i, tm, row0), 0)),
                  pl.BlockSpec((1, MLA_Q_RANK), lambda i: (0, 0)),
                  pl.BlockSpec((None, MLA_Q_RANK, MLA_HEADS * MLA_HEAD_PAD), lambda i: (l, 0, 0))],
        out_specs=[pl.BlockSpec((tm, MLA_HEADS * MLA_HEAD_PAD), lambda i: (i, 0)),
                   pl.BlockSpec((tm, 512), lambda i: (i, 0))],
        out_shape=[jax.ShapeDtypeStruct((nrows, MLA_HEADS * MLA_HEAD_PAD), BF16),
                   jax.ShapeDtypeStruct((nrows, 512), BF16)],
        compiler_params=_cparams(("arbitrary",)),
        name="q_prep",
    )(p, p, rope_cos, rope_sin, g_q, w_uq)


_NT = (((1,), (1,)), ((), ()))


def _softmax_parts(scores):
    m = scores[0].max(axis=-1, keepdims=True)
    for s in scores[1:]:
        m = jnp.maximum(m, s.max(axis=-1, keepdims=True))
    ps = [jnp.exp(s - m) for s in scores]
    tot = ps[0].sum(axis=-1, keepdims=True)
    for p in ps[1:]:
        tot = tot + p.sum(axis=-1, keepdims=True)
    return ps, 1.0 / tot


def _sub_tiles(n_rows):
    return [slice(r, r + TQ_SUB) for r in range(0, n_rows, TQ_SUB)]


def _mla_attn_kernel(n_seg, q_ref, *refs):
    k_refs, v_refs, o_ref = refs[:n_seg], refs[n_seg:2 * n_seg], refs[2 * n_seg]
    for rows in _sub_tiles(q_ref.shape[0]):
        q = q_ref[rows, :]
        scores = [lax.dot_general(q, k[...], _NT, preferred_element_type=F32) for k in k_refs]
        ps, inv = _softmax_parts(scores)
        acc = jnp.dot(ps[0].astype(BF16), v_refs[0][...], preferred_element_type=F32)
        for p, v in zip(ps[1:], v_refs[1:]):
            acc = acc + jnp.dot(p.astype(BF16), v[...], preferred_element_type=F32)
        o_ref[rows, :] = (acc * inv).astype(o_ref.dtype)


def _diff_attn_kernel(n_seg, lam_init, q_ref, lam_ref, g_ref, *refs):
    k_refs, v_refs, o_ref = refs[:n_seg], refs[n_seg:2 * n_seg], refs[2 * n_seg]
    lam = lam_ref[...]
    lam_val = (jnp.exp(jnp.sum(lam[0:1] * lam[1:2], axis=-1, keepdims=True))
               - jnp.exp(jnp.sum(lam[2:3] * lam[3:4], axis=-1, keepdims=True))) + lam_init
    for rows in _sub_tiles(q_ref.shape[0]):
        q = q_ref[rows, :]
        lane = lax.broadcasted_iota(jnp.int32, q.shape, 1)
        q1 = jnp.where(lane < DIFF_HD, q, jnp.zeros_like(q))
        q2 = jnp.where(lane < DIFF_HD, jnp.zeros_like(q), q)
        s1 = [lax.dot_general(q1, k[...], _NT, preferred_element_type=F32) for k in k_refs]
        s2 = [lax.dot_general(q2, k[...], _NT, preferred_element_type=F32) for k in k_refs]
        p1, inv1 = _softmax_parts(s1)
        p2, inv2 = _softmax_parts(s2)
        acc1 = acc2 = None
        for a, b, v in zip(p1, p2, v_refs):
            t1 = jnp.dot(a.astype(BF16), v[...], preferred_element_type=F32)
            t2 = jnp.dot(b.astype(BF16), v[...], preferred_element_type=F32)
            acc1, acc2 = (t1, t2) if acc1 is None else (acc1 + t1, acc2 + t2)
        acc = acc1 * inv1 - acc2 * (lam_val * inv2)
        o_ref[rows, :] = (_rms_rows(acc) * g_ref[...] * (1.0 - lam_init)).astype(o_ref.dtype)


def _attention(kind, q, q_col0, q_row0, n_q_per_batch, kv_segs, lam_init=None, lam=None, g_sub=None):
    heads, dq = (MLA_HEADS, MLA_HEAD_PAD) if kind == "mla" else (DIFF_HEADS, LANE)
    tq = min(TQ, n_q_per_batch)
    nq = n_q_per_batch // tq
    qb0 = q_row0 // tq
    qcb = q_col0 // dq
    n_seg = len(kv_segs)
    in_specs = [pl.BlockSpec((tq, dq), lambda b, h, i: (qb0 + b * nq + i, qcb + h))]
    args = [q]
    if kind == "diff":
        in_specs += [pl.BlockSpec((4, DIFF_HD), lambda b, h, i: (0, 0)),
                     pl.BlockSpec((1, DIFF_V), lambda b, h, i: (0, 0))]
        args += [lam, g_sub]
    for (k_arr, kcb, _, _, row0, rpb) in kv_segs:
        in_specs.append(pl.BlockSpec((rpb, dq), functools.partial(
            lambda b, h, i, r0, c0: (r0 + b, c0 + h), r0=row0 // rpb, c0=kcb)))
        args.append(k_arr)
    for (_, _, v_arr, vcb, row0, rpb) in kv_segs:
        in_specs.append(pl.BlockSpec((rpb, LANE), functools.partial(
            lambda b, h, i, r0, c0: (r0 + b, c0 + h), r0=row0 // rpb, c0=vcb)))
        args.append(v_arr)
    if kind == "mla":
        body = functools.partial(_mla_attn_kernel, n_seg)
    else:
        body = functools.partial(_diff_attn_kernel, n_seg, lam_init)
    return pl.pallas_call(
        body,
        grid=(BATCH, heads, nq),
        in_specs=in_specs,
        out_specs=pl.BlockSpec((tq, LANE), lambda b, h, i: (b * nq + i, h)),
        out_shape=jax.ShapeDtypeStruct((BATCH * n_q_per_batch, heads * LANE), BF16),
        compiler_params=_cparams(("arbitrary", "arbitrary", "arbitrary")),
        name=kind + "_attn",
    )(*args)


def _pool_kernel(seq, x_ref, w_ref, s_ref, o_ref):
    row = lax.broadcasted_iota(jnp.int32, (seq, POOL_GROUP), 0)

    def down(x, k):
        return jnp.where(row >= k, pltpu.roll(x, k, 0), 0.0)

    def up(x, k):
        return jnp.where(row < seq - k, pltpu.roll(x, seq - k, 0), 0.0)

    t = row.astype(F32)
    for gi, w in enumerate(POOL_WINDOWS):
        half = w // 2
        sl = slice(gi * POOL_GROUP, (gi + 1) * POOL_GROUP)
        x = x_ref[:, sl].astype(F32)
        before, after, span = down(x, 1), x, 1
        while span < half:
            before = before + down(before, span)
            after = after + up(after, span)
            span *= 2
        cnt = jnp.minimum(t + half, float(seq)) - jnp.maximum(t - half, 0.0)
        pooled = ((before + after) / cnt - x).astype(BF16)
        y = jnp.dot(pooled, w_ref[gi], preferred_element_type=F32)
        o_ref[:, sl] = (y * s_ref[:, sl]).astype(o_ref.dtype)


def _pool(p, row0, seq, w_pool, pool_scale, l):
    rb0 = row0 // seq
    return pl.pallas_call(
        functools.partial(_pool_kernel, seq),
        grid=(BATCH,),
        in_specs=[pl.BlockSpec((seq, POOL_W), lambda b: (rb0 + b, P_POOL // POOL_W)),
                  pl.BlockSpec((None, len(POOL_WINDOWS), POOL_GROUP, POOL_GROUP), lambda b: (l, 0, 0, 0)),
                  pl.BlockSpec((1, POOL_W), lambda b: (0, 0))],
        out_specs=pl.BlockSpec((seq, POOL_W), lambda b: (b, 0)),
        out_shape=jax.ShapeDtypeStruct((BATCH * seq, POOL_W), BF16),
        compiler_params=_cparams(("arbitrary",)),
        name="pool_mix",
    )(p, w_pool, pool_scale)


def _deepnorm(x, y, gate, g, b):
    return _layer_norm_rows(DN_ALPHA * x + gate * y) * g + b


def _merge_kernel(x_ref, gt_ref, om_ref, od_ref, op_ref, bg_ref, wm_ref, wd_ref, wp_ref, wo_ref,
                  g1_ref, lg_ref, lb_ref, o_ref):
    merged = None
    for br, (o_br, w_br) in enumerate(((om_ref, wm_ref), (od_ref, wd_ref), (op_ref, wp_ref))):
        y = jnp.dot(o_br[...], w_br[...], preferred_element_type=F32)
        gate = jax.nn.sigmoid(gt_ref[:, br * D:(br + 1) * D].astype(F32) + bg_ref[br:br + 1, :])
        merged = gate * y if merged is None else merged + gate * y
    out = jnp.dot(merged.astype(BF16), wo_ref[...], preferred_element_type=F32)
    o_ref[...] = _deepnorm(x_ref[...], out, g1_ref[0], lg_ref[...], lb_ref[...])


def _merge(x_all, p, o_mla, o_diff, o_pool, mods, b_gate, w_br_mla, w_br_diff, w_br_pool, w_o, ln_g, ln_b, l, nrows):
    tm = TM_MERGE
    const = lambda i: (0, 0)
    layer = lambda i: (l, 0, 0)
    once = dict(pipeline_mode=pl.Buffered(1))
    return pl.pallas_call(
        _merge_kernel,
        grid=(nrows // tm,),
        in_specs=[pl.BlockSpec((tm, D), lambda i: (i, 0)),
                  pl.BlockSpec((tm, 3 * D), lambda i: (i, P_GATE // (3 * D))),
                  pl.BlockSpec((tm, MLA_HEADS * MLA_V), lambda i: (i, 0)),
                  pl.BlockSpec((tm, DIFF_HEADS * DIFF_V), lambda i: (i, 0)),
                  pl.BlockSpec((tm, POOL_W), lambda i: (i, 0)),
                  pl.BlockSpec((3, D), const, **once),
                  pl.BlockSpec((None, MLA_HEADS * MLA_V, D), layer, **once),
                  pl.BlockSpec((None, DIFF_HEADS * DIFF_V, D), layer, **once),
                  pl.BlockSpec((None, POOL_W, D), layer, **once),
                  pl.BlockSpec((None, D, D), layer, **once),
                  pl.BlockSpec((1, 1, D), lambda i: (_mod_index(i, tm), 0, 2)),
                  pl.BlockSpec((1, D), const, **once),
                  pl.BlockSpec((1, D), const, **once)],
        out_specs=pl.BlockSpec((tm, D), lambda i: (i, 0)),
        out_shape=jax.ShapeDtypeStruct((nrows, D), F32),
        compiler_params=_cparams(("arbitrary",)),
        name="merge_out",
    )(x_all, p, o_mla, o_diff, o_pool, b_gate, w_br_mla, w_br_diff, w_br_pool, w_o, mods, ln_g, ln_b)


_HI_MASK = -65536


def _pack_bf16_pairs(lo, hi):
    lo_bits = lax.bitcast_convert_type(lo.astype(BF16).astype(F32), jnp.int32)
    hi_bits = lax.bitcast_convert_type(hi.astype(BF16).astype(F32), jnp.int32)
    return hi_bits | lax.shift_right_logical(lo_bits, jnp.full_like(lo_bits, 16))


def _unpack_bf16_pairs(w):
    lo = lax.bitcast_convert_type(jnp.left_shift(w, 16), F32)
    hi = lax.bitcast_convert_type(w & _HI_MASK, F32)
    return lo, hi


def _route_kernel(x_ref, sh_ref, sc_ref, w_ref, b_ref, vp_ref, r_ref, cnt_ref, carry_ref):
    @pl.when(pl.program_id(0) == 0)
    def _():
        carry_ref[...] = jnp.zeros_like(carry_ref)

    v = _layer_norm_rows(x_ref[...]) * (1.0 + sc_ref[0]) + sh_ref[0]
    vp_ref[...] = _pack_bf16_pairs(v[:, :D // 2], v[:, D // 2:])
    v_hi = v.astype(BF16)
    v_lo = (v - v_hi.astype(F32)).astype(BF16)
    t = jnp.dot(v_hi, w_ref[...], preferred_element_type=F32)
    logit = (t[:, :ROUTE_LANES] + t[:, ROUTE_LANES:]
             + jnp.dot(v_lo, w_ref[:, :ROUTE_LANES], preferred_element_type=F32) + b_ref[...])
    lane = lax.broadcasted_iota(jnp.int32, logit.shape, 1).astype(F32)
    neg = jnp.float32(-jnp.inf)
    big = jnp.float32(ROUTE_LANES)
    gl = jnp.where(lane < N_GROUPS, logit, neg)
    gmax = gl.max(axis=-1, keepdims=True)
    grp_p = 1.0 / jnp.exp(gl - gmax).sum(axis=-1, keepdims=True)
    grp = jnp.where(gl == gmax, lane, big).min(axis=-1, keepdims=True)
    grp_lo = N_GROUPS + EXPERTS_PER_GROUP * grp
    in_grp = (lane >= grp_lo) & (lane < grp_lo + EXPERTS_PER_GROUP)
    el = jnp.where(in_grp, logit, neg)
    t1 = el.max(axis=-1, keepdims=True)
    i1 = jnp.where(el == t1, lane, big).min(axis=-1, keepdims=True)
    el2 = jnp.where(lane == i1, neg, el)
    t2 = el2.max(axis=-1, keepdims=True)
    i2 = jnp.where(el2 == t2, lane, big).min(axis=-1, keepdims=True)
    e = jnp.exp(t2 - t1)
    w1 = grp_p / (1.0 + e)
    w2 = w1 * e
    oh1, oh2 = lane == i1, lane == i2
    oh = jnp.where(oh1 | oh2, 1.0, 0.0)
    tm = oh.shape[0]
    tri = (lax.broadcasted_iota(jnp.int32, (tm, tm), 1) < lax.broadcasted_iota(jnp.int32, (tm, tm), 0))
    before = jnp.dot(jnp.where(tri, 1.0, 0.0).astype(BF16), oh.astype(BF16),
                     preferred_element_type=F32) + carry_ref[...]
    rank1 = jnp.sum(jnp.where(oh1, before, 0.0), axis=-1, keepdims=True)
    rank2 = jnp.sum(jnp.where(oh2, before, 0.0), axis=-1, keepdims=True)
    carry_ref[...] += jnp.sum(oh, axis=0, keepdims=True)
    cnt_ref[...] = carry_ref[...]
    fields = (i1 - N_GROUPS, i2 - N_GROUPS, w1, w2, rank1, rank2)
    out = jnp.zeros_like(logit)
    for k, f in enumerate(fields):
        out = jnp.where(lane == k, f, out)
    r_ref[...] = out


def _route(x_all, mods, w_route, b_route, nrows):
    tm = TM_ROUTE
    return pl.pallas_call(
        _route_kernel,
        grid=(nrows // tm,),
        in_specs=[pl.BlockSpec((tm, D), lambda i: (i, 0)),
                  pl.BlockSpec((1, 1, D), lambda i: (_mod_index(i, tm), 0, 3)),
                  pl.BlockSpec((1, 1, D), lambda i: (_mod_index(i, tm), 0, 4)),
                  pl.BlockSpec((D, 2 * ROUTE_LANES), lambda i: (0, 0)),
                  pl.BlockSpec((1, ROUTE_LANES), lambda i: (0, 0))],
        out_specs=[pl.BlockSpec((tm, D // 2), lambda i: (i, 0)),
                   pl.BlockSpec((tm, ROUTE_LANES), lambda i: (i, 0)),
                   pl.BlockSpec((1, ROUTE_LANES), lambda i: (0, 0))],
        out_shape=[jax.ShapeDtypeStruct((nrows, D // 2), jnp.int32),
                   jax.ShapeDtypeStruct((nrows, ROUTE_LANES), F32),
                   jax.ShapeDtypeStruct((1, ROUTE_LANES), F32)],
        scratch_shapes=[pltpu.VMEM((1, ROUTE_LANES), F32)],
        compiler_params=_cparams(("arbitrary",)),
        name="moe_route",
    )(x_all, mods, mods, w_route, b_route)


def _gather_rows(src_hbm, idx_ref, base, n, dst, sem):
    def body(r, carry):
        tok = idx_ref[base + r]
        pltpu.make_async_copy(src_hbm.at[pl.ds(tok, 1), :], dst.at[pl.ds(r, 1), :], sem).start()
        return carry
    lax.fori_loop(0, n, body, 0, unroll=8)


def _wait_rows(src_hbm, n, dst, sem):
    pltpu.make_async_copy(src_hbm.at[pl.ds(0, n), :], dst, sem).wait()


def _dispatch_kernel(dest_ref, zrow_ref, nused_ref, vp_ref, xs_hbm, zbuf, sems):
    i = pl.program_id(0)
    tm = vp_ref.shape[0]
    row_sem, zero_sem = sems.at[0], sems.at[1]
    n_blocks = xs_hbm.shape[0] // EXPERT_ROWS

    @pl.when(i == 0)
    def _():
        zbuf[...] = jnp.zeros_like(zbuf)

        def clear(row):
            pltpu.make_async_copy(zbuf, xs_hbm.at[pl.ds(pl.multiple_of(row, EXPERT_ROWS), EXPERT_ROWS), :],
                                  zero_sem).start()

        def wait_clear():
            pltpu.make_async_copy(zbuf, xs_hbm.at[pl.ds(0, EXPERT_ROWS), :], zero_sem).wait()

        def fill(e, carry):
            @pl.when(zrow_ref[e] >= 0)
            def _():
                clear(zrow_ref[e])
            return carry
        lax.fori_loop(0, N_EXPERTS, fill, 0)

        def fill_tail(b, carry):
            clear(b * EXPERT_ROWS)
            return carry
        lax.fori_loop(nused_ref[0], n_blocks, fill_tail, 0)

        def drain(e, carry):
            @pl.when(zrow_ref[e] >= 0)
            def _():
                wait_clear()
            return carry
        lax.fori_loop(0, N_EXPERTS, drain, 0)

        def drain_tail(b, carry):
            wait_clear()
            return carry
        lax.fori_loop(nused_ref[0], n_blocks, drain_tail, 0)

    def scatter(r, carry):
        for k in range(TOP_K):
            slot = dest_ref[(i * TOP_K + k) * tm + r]
            pltpu.make_async_copy(vp_ref.at[pl.ds(r, 1), :], xs_hbm.at[pl.ds(slot, 1), :], row_sem).start()
        return carry
    lax.fori_loop(0, tm, scatter, 0, unroll=8)
    for _ in range(TOP_K):
        pltpu.make_async_copy(vp_ref, xs_hbm.at[pl.ds(0, tm), :], row_sem).wait()


def _dispatch(vp, dest, zrow, n_used, n_rows):
    tm = TM_COMB
    grid_spec = pltpu.PrefetchScalarGridSpec(
        num_scalar_prefetch=3,
        grid=(vp.shape[0] // tm,),
        in_specs=[pl.BlockSpec((tm, D // 2), lambda i, d, z, nu: (i, 0))],
        out_specs=pl.BlockSpec(memory_space=pl.ANY),
        scratch_shapes=[pltpu.VMEM((EXPERT_ROWS, D // 2), jnp.int32),
                        pltpu.SemaphoreType.DMA((2,))],
    )
    return pl.pallas_call(
        _dispatch_kernel,
        grid_spec=grid_spec,
        out_shape=jax.ShapeDtypeStruct((n_rows, D // 2), jnp.int32),
        compiler_params=_cparams(("arbitrary",)),
        name="moe_dispatch",
    )(dest, zrow, n_used, vp)


def _ffn_kernel(l, bexp_ref, nused_ref, next_ref, xs_ref, wgu_hbm, wdn_hbm, o_ref,
                wgu_f, wdn_f, wgu_bf, wdn_bf, slot_ref, sems):
    i = pl.program_id(0)
    n_used = nused_ref[0]
    e = bexp_ref[i]

    def weight_copies(expert, slot):
        return (pltpu.make_async_copy(wgu_hbm.at[l, expert], wgu_f.at[slot], sems.at[slot, 0]),
                pltpu.make_async_copy(wdn_hbm.at[l, expert], wdn_f.at[slot], sems.at[slot, 1]))

    @pl.when(i == 0)
    def _():
        slot_ref[0] = 0
        for cp in weight_copies(e, 0):
            cp.start()

    @pl.when((i == 0) | (e != bexp_ref[jnp.maximum(i - 1, 0)]))
    def _():
        slot = slot_ref[0]
        for cp in weight_copies(e, slot):
            cp.wait()
        nxt = next_ref[e]

        @pl.when(nxt >= 0)
        def _():
            for cp in weight_copies(nxt, 1 - slot):
                cp.start()

        wgu_bf[...] = wgu_f[slot].astype(BF16)
        wdn_bf[...] = wdn_f[slot].astype(BF16)
        slot_ref[0] = 1 - slot

    @pl.when(i < n_used)
    def _():
        lo, hi = _unpack_bf16_pairs(xs_ref[...])
        h = (jnp.dot(lo.astype(BF16), wgu_bf[:D // 2, :], preferred_element_type=F32)
             + jnp.dot(hi.astype(BF16), wgu_bf[D // 2:, :], preferred_element_type=F32))
        gate, up = h[:, :D_EXPERT], h[:, D_EXPERT:]
        a = (gate * jax.nn.sigmoid(gate) * up).astype(BF16)
        y = jnp.dot(a, wdn_bf[...], preferred_element_type=F32)
        o_ref[...] = _pack_bf16_pairs(y[:, :D // 2], y[:, D // 2:])

    @pl.when(i >= n_used)
    def _():
        o_ref[...] = jnp.zeros_like(o_ref)


def _expert_ffn(xs, block_exp, n_used, next_exp, w_gu, w_dn, l):
    n_blocks = block_exp.shape[0]
    grid_spec = pltpu.PrefetchScalarGridSpec(
        num_scalar_prefetch=3,
        grid=(n_blocks,),
        in_specs=[pl.BlockSpec((EXPERT_ROWS, D // 2), lambda i, be, nu, nx: (jnp.minimum(i, nu[0] - 1), 0)),
                  pl.BlockSpec(memory_space=pl.ANY),
                  pl.BlockSpec(memory_space=pl.ANY)],
        out_specs=pl.BlockSpec((EXPERT_ROWS, D // 2), lambda i, be, nu, nx: (i, 0)),
        scratch_shapes=[pltpu.VMEM((2, D, 2 * D_EXPERT), F32),
                        pltpu.VMEM((2, D_EXPERT, D), F32),
                        pltpu.VMEM((D, 2 * D_EXPERT), BF16),
                        pltpu.VMEM((D_EXPERT, D), BF16),
                        pltpu.SMEM((1,), jnp.int32),
                        pltpu.SemaphoreType.DMA((2, 2))],
    )
    return pl.pallas_call(
        functools.partial(_ffn_kernel, l),
        grid_spec=grid_spec,
        out_shape=jax.ShapeDtypeStruct((n_blocks * EXPERT_ROWS, D // 2), jnp.int32),
        compiler_params=_cparams(("arbitrary",)),
        name="expert_ffn",
    )(block_exp, n_used, next_exp, xs, w_gu, w_dn)


def _combine_kernel(dest_ref, y_hbm, x_ref, r_ref, g2_ref, lg_ref, lb_ref, o_ref, ybuf, sems):
    i = pl.program_id(0)
    n = pl.num_programs(0)
    tm = x_ref.shape[0]
    cur = i % 2

    @pl.when(i == 0)
    def _():
        _gather_rows(y_hbm, dest_ref, 0, TOP_K * tm, ybuf.at[0], sems.at[0])

    @pl.when(i + 1 < n)
    def _():
        _gather_rows(y_hbm, dest_ref, (i + 1) * TOP_K * tm, TOP_K * tm, ybuf.at[1 - cur], sems.at[1 - cur])

    _wait_rows(y_hbm, TOP_K * tm, ybuf.at[cur], sems.at[cur])
    r = r_ref[...]
    w1, w2 = r[:, 2:3], r[:, 3:4]
    lo1, hi1 = _unpack_bf16_pairs(ybuf[cur, 0:tm, :])
    lo2, hi2 = _unpack_bf16_pairs(ybuf[cur, tm:2 * tm, :])
    y = jnp.concatenate([lo1 * w1 + lo2 * w2, hi1 * w1 + hi2 * w2], axis=-1)
    o_ref[...] = _deepnorm(x_ref[...], y, g2_ref[0], lg_ref[...], lb_ref[...])


def _combine(x_all, yb, dest, route, mods, ln_g, ln_b, nrows):
    tm = TM_COMB
    grid_spec = pltpu.PrefetchScalarGridSpec(
        num_scalar_prefetch=1,
        grid=(nrows // tm,),
        in_specs=[pl.BlockSpec(memory_space=pl.ANY),
                  pl.BlockSpec((tm, D), lambda i, d: (i, 0)),
                  pl.BlockSpec((tm, ROUTE_LANES), lambda i, d: (i, 0)),
                  pl.BlockSpec((1, 1, D), lambda i, d: (_mod_index(i, tm), 0, 5)),
                  pl.BlockSpec((1, D), lambda i, d: (0, 0)),
                  pl.BlockSpec((1, D), lambda i, d: (0, 0))],
        out_specs=pl.BlockSpec((tm, D), lambda i, d: (i, 0)),
        scratch_shapes=[pltpu.VMEM((2, TOP_K * tm, D // 2), jnp.int32),
                        pltpu.SemaphoreType.DMA((2,))],
    )
    return pl.pallas_call(
        _combine_kernel,
        grid_spec=grid_spec,
        out_shape=jax.ShapeDtypeStruct((nrows, D), F32),
        compiler_params=_cparams(("arbitrary",)),
        name="moe_combine",
    )(dest, yb, x_all, route, mods, ln_g, ln_b)


def _num_expert_blocks(n_tok):
    return -(-(n_tok * TOP_K + N_EXPERTS * (EXPERT_ROWS - 1)) // EXPERT_ROWS)


def _dispatch_indices(route, counts_row, n_tok):
    experts = jnp.arange(N_EXPERTS, dtype=jnp.int32)
    counts = counts_row[0, N_GROUPS:N_GROUPS + N_EXPERTS].astype(jnp.int32)
    padded = (counts + EXPERT_ROWS - 1) // EXPERT_ROWS * EXPERT_ROWS
    pad_end = jnp.cumsum(padded)
    pad_start = pad_end - padded
    n_used = (pad_end[-1] // EXPERT_ROWS).astype(jnp.int32)
    blk = jnp.arange(_num_expert_blocks(n_tok), dtype=jnp.int32)
    block_exp = jnp.sum((pad_end[None, :] <= (blk * EXPERT_ROWS)[:, None]).astype(jnp.int32), axis=1)
    last_exp = jnp.max(jnp.where(counts > 0, experts, 0))
    block_exp = jnp.where(blk < n_used, jnp.minimum(block_exp, N_EXPERTS - 1), last_exp).astype(jnp.int32)
    zrow = jnp.where(counts > 0, pad_end - EXPERT_ROWS, -1).astype(jnp.int32)
    later = (experts[None, :] > experts[:, None]) & (counts > 0)[None, :]
    next_exp = jnp.min(jnp.where(later, experts[None, :], N_EXPERTS), axis=1)
    next_exp = jnp.where(next_exp < N_EXPERTS, next_exp, -1).astype(jnp.int32)
    e = route[:, 0:TOP_K].astype(jnp.int32)
    rank = route[:, 4:4 + TOP_K].astype(jnp.int32)
    base = jnp.sum(jnp.where(e[..., None] == experts, pad_start, 0), axis=-1)
    dest = (rank + base).astype(jnp.int32)
    dest_blocks = dest.reshape(n_tok // TM_COMB, TM_COMB, TOP_K).transpose(0, 2, 1).reshape(n_tok * TOP_K)
    return block_exp, n_used.reshape(1), dest_blocks, zrow, next_exp


_IN_OFFS = dict(zip(("c_kv", "k_rot", "k_diff", "v_diff", "c_q", "q_diff", "pool", "gate"),
                    np.cumsum((0, MLA_KV_RANK, MLA_ROPE, 512, 512, MLA_Q_RANK, 512, POOL_W)).tolist()))
_IN_MOVES = ((_IN_OFFS["gate"], 3 * D, P_GATE), (_IN_OFFS["c_q"], MLA_Q_RANK, P_CQ),
             (_IN_OFFS["c_kv"], MLA_KV_RANK, P_CKV), (_IN_OFFS["k_diff"], 512, P_KD),
             (_IN_OFFS["v_diff"], 512, P_VD), (_IN_OFFS["q_diff"], 512, P_QD), (_IN_OFFS["pool"], POOL_W, P_POOL))
IN_COLS = _IN_OFFS["gate"] + 3 * D


def _relayout_in_kernel(w_ref, o_ref):
    for src, width, dst in _IN_MOVES:
        o_ref[:, dst:dst + width] = w_ref[:, src:src + width].astype(BF16)
    kr = w_ref[:, _IN_OFFS["k_rot"]:_IN_OFFS["k_rot"] + LANE]
    lane = lax.broadcasted_iota(jnp.int32, kr.shape, 1)
    o_ref[:, P_KROT:P_KROT + LANE] = jnp.where(lane < MLA_ROPE, kr, 0.0).astype(BF16)
    o_ref[:, P_KROT + LANE:P_CKV] = jnp.zeros((kr.shape[0], P_CKV - P_KROT - LANE), BF16)


def _relayout_w_in(w_in):
    tr = 256
    return pl.pallas_call(
        _relayout_in_kernel,
        grid=(DEPTH, D // tr),
        in_specs=[pl.BlockSpec((None, tr, IN_COLS), lambda l, i: (l, i, 0))],
        out_specs=pl.BlockSpec((None, tr, P_W), lambda l, i: (l, i, 0)),
        out_shape=jax.ShapeDtypeStruct((DEPTH, D, P_W), BF16),
        compiler_params=_cparams(("arbitrary", "arbitrary")),
        name="relayout_w_in",
    )(w_in)


def _relayout_w_uq(w_uq):
    w = w_uq.reshape(DEPTH, MLA_Q_RANK, MLA_HEADS, MLA_NOPE + MLA_ROPE)
    w = jnp.pad(w, ((0, 0), (0, 0), (0, 0), (0, MLA_HEAD_PAD - MLA_NOPE - MLA_ROPE)))
    return w.reshape(DEPTH, MLA_Q_RANK, MLA_HEADS * MLA_HEAD_PAD).astype(BF16)


def _relayout_w_ukv(w_ukv):
    w = w_ukv.reshape(DEPTH, MLA_KV_RANK, MLA_HEADS, MLA_NOPE + MLA_V)
    k = w[..., :MLA_NOPE].reshape(DEPTH, MLA_KV_RANK, MLA_HEADS * MLA_NOPE)
    v = w[..., MLA_NOPE:].reshape(DEPTH, MLA_KV_RANK, MLA_HEADS * MLA_V)
    return jnp.concatenate([k, v], axis=-1).astype(BF16)


def _relayout_router(w_grp, b_grp, w_exp, b_exp):
    w = jnp.concatenate([w_grp, w_exp], axis=-1)
    w = jnp.pad(w, ((0, 0), (0, 0), (0, ROUTE_LANES - w.shape[-1])))
    w_hi = w.astype(BF16)
    w_lo = (w - w_hi.astype(F32)).astype(BF16)
    b = jnp.pad(jnp.concatenate([b_grp, b_exp], axis=-1), ((0, 0), (0, ROUTE_LANES - N_GROUPS - N_EXPERTS)))
    return jnp.concatenate([w_hi, w_lo], axis=-1), b.reshape(DEPTH, 1, ROUTE_LANES)


def _rope_tables():
    quarter = MLA_ROPE // 4
    inv_freq = ROPE_BASE ** (-jnp.arange(quarter, dtype=F32) / quarter)
    pos = jnp.arange(SEQ)
    row = (pos // GRID_W).astype(F32)[:, None] * inv_freq
    col = (pos % GRID_W).astype(F32)[:, None] * inv_freq
    cos64 = jnp.concatenate([jnp.cos(row), jnp.cos(row), jnp.cos(col), jnp.cos(col)], axis=-1)
    sin64 = jnp.concatenate([-jnp.sin(row), jnp.sin(row), -jnp.sin(col), jnp.sin(col)], axis=-1)
    cos = jnp.concatenate([jnp.tile(cos64, (1, 2)), jnp.ones((TM_PREP, LANE), F32)], axis=0)
    sin = jnp.concatenate([jnp.tile(sin64, (1, 2)), jnp.zeros((TM_PREP, LANE), F32)], axis=0)
    return cos, sin


def kernel(x, c, ctx, c_ctx, w_ada, b_ada, w_in, b_gate, g_q, g_kv, w_uq, w_ukv, lam, g_sub, w_pool, pool_scale,
           w_br_mla, w_br_diff, w_br_pool, w_o, ln1_g, ln1_b, w_grp, b_grp, w_exp, b_exp, w_gu, w_dn, ln2_g, ln2_b):
    assert x.shape == (BATCH, SEQ, D) and ctx.shape == (BATCH, CTX, D)
    assert MLA_ROPE == DIFF_HD
    cc = jnp.concatenate([c, c_ctx[None], jnp.zeros((MOD_ROWS - BATCH - 1, D), F32)], axis=0)
    ada = _adaln(cc, w_ada, b_ada)
    w_in_p = _relayout_w_in(w_in)
    w_uq_p = _relayout_w_uq(w_uq)
    w_ukv_p = _relayout_w_ukv(w_ukv)
    w_route, b_route = _relayout_router(w_grp, b_grp, w_exp, b_exp)
    rope_cos, rope_sin = _rope_tables()
    bf = lambda a: a.astype(BF16)
    w_pool_b, w_br_mla_b, w_br_diff_b, w_br_pool_b, w_o_b = map(bf, (w_pool, w_br_mla, w_br_diff, w_br_pool, w_o))

    x_all = jnp.concatenate([x.reshape(N_LAT, D), ctx.reshape(N_CTX, D)], axis=0)
    for l in range(DEPTH):
        last = l == DEPTH - 1
        lam_init = 0.8 - 0.6 * math.exp(-0.3 * l)
        mods = ada[l].reshape(MOD_ROWS, 1, 6 * D)
        n_out = N_LAT if last else N_ALL
        g_kv_l, g_q_l = g_kv[l].reshape(1, -1), g_q[l].reshape(1, -1)
        g_sub_l, ps_l = g_sub[l].reshape(1, -1), pool_scale[l].reshape(1, -1)

        p = _proj(x_all, mods, w_in_p, l, 0, n_out, 0, P_W)
        k_mla, v_mla, k_diff = _kvprep(p, 0, 0, n_out, rope_cos, rope_sin, g_kv_l, w_ukv_p, l)
        q_mla, q_diff = _qprep(p, 0, n_out, rope_cos, rope_sin, g_q_l, w_uq_p, l)
        if last:
            p_c = _proj(x_all, mods, w_in_p, l, N_LAT, N_CTX, P_KV0, P_KV_W)
            k_mla_c, v_mla_c, k_diff_c = _kvprep(p_c, P_KV0, N_LAT, N_CTX, rope_cos, rope_sin, g_kv_l, w_ukv_p, l)
            c_row0, pv_c, pv_c_col = 0, p_c, (P_VD - P_KV0) // LANE
        else:
            k_mla_c, v_mla_c, k_diff_c, c_row0, pv_c, pv_c_col = k_mla, v_mla, k_diff, N_LAT, p, P_VD // LANE
        mla_segs = [(k_mla, 0, v_mla, 0, 0, SEQ), (k_mla_c, 0, v_mla_c, 0, c_row0, CTX)]
        diff_segs = [(k_diff, 0, p, P_VD // LANE, 0, SEQ), (k_diff_c, 0, pv_c, pv_c_col, c_row0, CTX)]
        diff_kw = dict(lam_init=lam_init, lam=lam[l], g_sub=g_sub_l)
        o_mla = _attention("mla", q_mla, 0, 0, SEQ, mla_segs)
        o_diff = _attention("diff", q_diff, 0, 0, SEQ, diff_segs, **diff_kw)
        o_pool = _pool(p, 0, SEQ, w_pool_b, ps_l, l)
        if not last:
            o_mla = jnp.concatenate([o_mla, _attention("mla", q_mla, 0, N_LAT, CTX, mla_segs[1:])], axis=0)
            o_diff = jnp.concatenate([o_diff, _attention("diff", q_diff, 0, N_LAT, CTX, diff_segs[1:], **diff_kw)], axis=0)
            o_pool = jnp.concatenate([o_pool, _pool(p, N_LAT, CTX, w_pool_b, ps_l, l)], axis=0)
        x_all = _merge(x_all, p, o_mla, o_diff, o_pool, mods, b_gate[l], w_br_mla_b, w_br_diff_b,
                       w_br_pool_b, w_o_b, ln1_g[l].reshape(1, D), ln1_b[l].reshape(1, D), l, n_out)

        vp, route, counts = _route(x_all, mods, w_route[l], b_route[l], n_out)
        block_exp, n_used, dest, zrow, next_exp = _dispatch_indices(route, counts, n_out)
        xs = _dispatch(vp, dest, zrow, n_used, _num_expert_blocks(n_out) * EXPERT_ROWS)
        yb = _expert_ffn(xs, block_exp, n_used, next_exp, w_gu, w_dn, l)
        x_all = _combine(x_all, yb, dest, route, mods, ln2_g[l].reshape(1, D), ln2_b[l].reshape(1, D), n_out)
    return x_all.reshape(BATCH, SEQ, D)
```
